```python
import math
import jax
import jax.numpy as jnp
from jax import lax
import numpy as np

D_MODEL = 1024
BATCH = 2
SEQ = 8192
DEPTH = 4


N_MIXERS = 4
EPS = 1e-6
MACARON_W = 0.5
D_FF = 2816
ROPE_BASE = 10000.0
REL_BUCKETS = 32
REL_MAX_DIST = 128
MLA_HEADS = 16
MLA_Q_RANK = 384
MLA_KV_RANK = 256
MLA_NOPE = 64
MLA_ROPE = 32
MLA_V = 64
MLA_Q_BLOCK = 128
RWKV_HEAD = 64
RWKV_HEADS = D_MODEL // RWKV_HEAD
RWKV_DECAY_LORA = 64
RWKV_A_LORA = 64
RWKV_GATE_LORA = 160
RWKV_GN_EPS = 64e-5
MOBA_HEADS = 16
MOBA_HEAD_DIM = D_MODEL // MOBA_HEADS
MOBA_BLOCK = 256
MOBA_TOPK = 3
MOBA_Q_CHUNK = 32
RET_HEADS = 4
RET_DK = D_MODEL // RET_HEADS
RET_DV = 2 * D_MODEL // RET_HEADS
RET_CHUNK = 128
RET_GN_EPS = 1e-5

kernel_name = 'hybrid_mla_rwkv7_moba_retnet_macaron_adaln'


def _rmsnorm(x, g):
    xf = x.astype(jnp.float32)
    y = xf * lax.rsqrt(jnp.mean(xf * xf, axis=-1, keepdims=True) + EPS)
    return (y * g.astype(jnp.float32)).astype(x.dtype)


def _modulate(x, g, shift, scale):
    return _rmsnorm(x, g) * (1 + scale[:, None, :]) + shift[:, None, :]


def _groupnorm_heads(y, w, b, eps):
    B, S, H, N = y.shape
    yf = y.astype(jnp.float32)
    mu = jnp.mean(yf, axis=-1, keepdims=True)
    var = jnp.mean(jnp.square(yf - mu), axis=-1, keepdims=True)
    yn = ((yf - mu) * lax.rsqrt(var + eps)).reshape(B, S, H * N)
    return yn * w.astype(jnp.float32) + b.astype(jnp.float32)


def _swiglu(h, w_in, w_out):
    gt, up = jnp.split(h @ w_in, 2, axis=-1)
    return (jax.nn.silu(gt) * up) @ w_out


def _rope_tables(seq, dim):
    inv = ROPE_BASE ** (-jnp.arange(0, dim, 2, dtype=jnp.float32) / dim)
    ang = jnp.arange(seq, dtype=jnp.float32)[:, None] * inv[None, :]
    return jnp.cos(ang), jnp.sin(ang)


def _apply_rope(x, cos, sin):
    half = x.shape[-1] // 2
    x1 = x[..., :half].astype(jnp.float32)
    x2 = x[..., half:].astype(jnp.float32)
    c = cos[None, :, None, :]
    s = sin[None, :, None, :]
    return jnp.concatenate([x1 * c - x2 * s, x1 * s + x2 * c], axis=-1).astype(x.dtype)


def _t5_bucket(dist):
    n = jnp.maximum(dist, 0)
    max_exact = REL_BUCKETS // 2
    nf = jnp.maximum(n, max_exact).astype(jnp.float32)
    large = max_exact + (jnp.log(nf / max_exact) / math.log(REL_MAX_DIST / max_exact)
                         * (REL_BUCKETS - max_exact)).astype(jnp.int32)
    large = jnp.minimum(large, REL_BUCKETS - 1)
    return jnp.where(n < max_exact, n, large)


def _dense_causal_attention(q, k, v):
    B, S, H, dq = q.shape
    dv = v.shape[-1]
    nq = S // MLA_Q_BLOCK
    qb = q.reshape(B, nq, MLA_Q_BLOCK, H, dq).transpose(1, 0, 3, 2, 4)
    kpos = jnp.arange(S)

    def one_block(args):
        qi, bi = args
        s = jnp.einsum('bhqd,bkhd->bhqk', qi, k).astype(jnp.float32)
        qpos = bi * MLA_Q_BLOCK + jnp.arange(MLA_Q_BLOCK)
        s = jnp.where(kpos[None, :] <= qpos[:, None], s, -jnp.inf)
        p = jax.nn.softmax(s, axis=-1).astype(v.dtype)
        return jnp.einsum('bhqk,bkhd->bqhd', p, v)

    o = lax.map(one_block, (qb, jnp.arange(nq)))
    return o.transpose(1, 0, 2, 3, 4).reshape(B, S, H * dv)


def _mla_mixer(h, w_in, q_norm, w_uq, kv_norm, w_ukv, w_out):
    B, S, _ = h.shape
    H = MLA_HEADS
    cq, ckv, k_rope = jnp.split(h @ w_in, [MLA_Q_RANK, MLA_Q_RANK + MLA_KV_RANK], axis=-1)
    q = (_rmsnorm(cq, q_norm) @ w_uq).reshape(B, S, H, MLA_NOPE + MLA_ROPE)
    kv = (_rmsnorm(ckv, kv_norm) @ w_ukv).reshape(B, S, H, MLA_NOPE + MLA_V)
    cos, sin = _rope_tables(S, MLA_ROPE)
    q = jnp.concatenate([q[..., :MLA_NOPE], _apply_rope(q[..., MLA_NOPE:], cos, sin)], axis=-1)
    k_rope = jnp.broadcast_to(_apply_rope(k_rope[:, :, None, :], cos, sin), (B, S, H, MLA_ROPE))
    k = jnp.concatenate([kv[..., :MLA_NOPE], k_rope], axis=-1)
    v = kv[..., MLA_NOPE:]
    o = _dense_causal_attention(q * (MLA_NOPE + MLA_ROPE) ** -0.5, k, v)
    return o @ w_out


def _rwkv7_mixer(h, mu, w_rkv, w0, wd1, wd2, a0, wa1, wa2, wg1, wg2, k_k, k_a, r_k, gn_w, gn_b, w_out):
    B, S, D = h.shape
    H, N = RWKV_HEADS, RWKV_HEAD
    xx = jnp.pad(h, ((0, 0), (1, 0), (0, 0)))[:, :-1] - h
    xs = h[:, :, None, :] + xx[:, :, None, :] * mu
    rkv = jnp.einsum('bsnd,nde->bsne', xs[:, :, :3], w_rkv)
    r, k, v = rkv[:, :, 0], rkv[:, :, 1], rkv[:, :, 2]
    xw, xa, xg = xs[:, :, 3], xs[:, :, 4], xs[:, :, 5]
    w = -jax.nn.softplus(-(w0 + jnp.tanh(xw @ wd1) @ wd2)) - 0.5
    decay = jnp.exp(-jnp.exp(w.astype(jnp.float32)))
    a = jax.nn.sigmoid((a0 + (xa @ wa1) @ wa2).astype(jnp.float32))
    g = jax.nn.sigmoid(xg @ wg1) @ wg2
    kk = (k * k_k).astype(jnp.float32).reshape(B, S, H, N)
    kk = kk / jnp.maximum(jnp.sqrt(jnp.sum(kk * kk, axis=-1, keepdims=True)), 1e-12)
    k_mod = k.astype(jnp.float32) * (1 + (a - 1) * k_a.astype(jnp.float32))

    def heads(t):
        return t.astype(jnp.float32).reshape(B, S, H, N)

    r_h, k_h, v_h, a_h, w_h = heads(r), heads(k_mod), heads(v), heads(a), heads(decay)

    def tm(t):
        return t.transpose(1, 0, 2, 3)

    def step(state, inp):
        r_t, w_t, k_t, v_t, ka_t, kb_t = inp
        sa = jnp.einsum('bhvk,bhk->bhv', state, ka_t)
        state = (state * w_t[:, :, None, :] + sa[..., None] * kb_t[:, :, None, :]
                 + v_t[..., None] * k_t[:, :, None, :])
        return state, jnp.einsum('bhvk,bhk->bhv', state, r_t)

    state0 = jnp.zeros((B, H, N, N), jnp.float32)
    _, y = lax.scan(step, state0, (tm(r_h), tm(w_h), tm(k_h), tm(v_h), tm(-kk), tm(kk * a_h)))
    y = y.transpose(1, 0, 2, 3)
    yn = _groupnorm_heads(y, gn_w, gn_b, RWKV_GN_EPS)
    bonus = (jnp.sum(r_h * k_h * r_k.astype(jnp.float32), axis=-1, keepdims=True) * v_h).reshape(B, S, D)
    return ((yn + bonus) * g.astype(jnp.float32)).astype(h.dtype) @ w_out


def _moba_mixer(h, w_in, rel_table, w_out):
    B, S, _ = h.shape
    H, dh, BLK, QC = MOBA_HEADS, MOBA_HEAD_DIM, MOBA_BLOCK, MOBA_Q_CHUNK
    qkv = (h @ w_in).reshape(B, S, 3, H, dh)
    q = qkv[:, :, 0].transpose(0, 2, 1, 3) * dh ** -0.5
    k = qkv[:, :, 1].transpose(0, 2, 1, 3)
    v = qkv[:, :, 2].transpose(0, 2, 1, 3)
    nb = -(-S // BLK)
    K = min(MOBA_TOPK, nb)
    pad = nb * BLK - S
    kb = jnp.pad(k, ((0, 0), (0, 0), (0, pad), (0, 0))).reshape(B, H, nb, BLK, dh)
    vb = jnp.pad(v, ((0, 0), (0, 0), (0, pad), (0, 0))).reshape(B, H, nb, BLK, dh)
    kmean = jnp.mean(kb.astype(jnp.float32), axis=3)
    table_t = rel_table.astype(jnp.float32).T
    nc = S // QC
    qc = q.reshape(B, H, nc, QC, dh).transpose(2, 0, 1, 3, 4)
    b_ix = jnp.arange(B)[:, None, None, None]
    h_ix = jnp.arange(H)[None, :, None, None]
    blk_ids = jnp.arange(nb)
    offs = jnp.arange(BLK)

    def chunk(args):
        qi, ci = args
        start = ci * QC
        qpos = start + jnp.arange(QC)
        own = start // BLK
        gate = jnp.einsum('bhqd,bhnd->bhqn', qi.astype(jnp.float32), kmean)
        gate = jnp.where(blk_ids < own, gate, -jnp.inf)
        top_s, top_i = lax.top_k(gate, K)
        valid = jnp.isfinite(top_s)
        k_sel = kb[b_ix, h_ix, top_i]
        v_sel = vb[b_ix, h_ix, top_i]
        pos_sel = top_i[..., None] * BLK + offs
        bias_sel = table_t[h_ix[..., None], _t5_bucket(qpos[:, None, None] - pos_sel)]
        s_sel = jnp.einsum('bhqd,bhqjkd->bhqjk', qi, k_sel).astype(jnp.float32) + bias_sel
        s_sel = jnp.where(valid[..., None], s_sel, -jnp.inf).reshape(B, H, QC, K * BLK)
        k_own = lax.dynamic_index_in_dim(kb, own, axis=2, keepdims=False)
        v_own = lax.dynamic_index_in_dim(vb, own, axis=2, keepdims=False)
        dist_own = qpos[:, None] - (own * BLK + offs)[None, :]
        s_own = jnp.einsum('bhqd,bhkd->bhqk', qi, k_own).astype(jnp.float32) + table_t[:, _t5_bucket(dist_own)]
        s_own = jnp.where(dist_own >= 0, s_own, -jnp.inf)
        p = jax.nn.softmax(jnp.concatenate([s_sel, s_own], axis=-1), axis=-1).astype(v.dtype)
        p_sel = p[..., :K * BLK].reshape(B, H, QC, K, BLK)
        p_own = p[..., K * BLK:]
        return (jnp.einsum('bhqjk,bhqjkd->bhqd', p_sel, v_sel)
                + jnp.einsum('bhqk,bhkd->bhqd', p_own, v_own))

    o = lax.map(chunk, (qc, jnp.arange(nc)))
    o = o.transpose(1, 0, 3, 2, 4).reshape(B, S, H * dh)
    return o @ w_out


def _retnet_mixer(h, w_in, gn_w, gn_b, w_out):
    B, S, D = h.shape
    H, DK, DV, C = RET_HEADS, RET_DK, RET_DV, RET_CHUNK
    q, k, v, g = jnp.split(h @ w_in, [D, 2 * D, 4 * D], axis=-1)
    cos, sin = _rope_tables(S, DK)
    q = _apply_rope(q.reshape(B, S, H, DK), cos, sin).astype(jnp.float32)
    k = _apply_rope(k.reshape(B, S, H, DK) * DK ** -0.5, cos, sin).astype(jnp.float32)
    v = v.reshape(B, S, H, DV).astype(jnp.float32)
    log_gamma = jnp.log1p(-jnp.exp2(-5.0 - jnp.arange(H, dtype=jnp.float32)))
    idx = jnp.arange(C, dtype=jnp.float32)
    diff = idx[:, None] - idx[None, :]
    dmask = jnp.where(diff >= 0, jnp.exp(jnp.maximum(diff, 0.0)[None] * log_gamma[:, None, None]), 0.0)
    zeta = jnp.exp((C - 1 - idx)[None, :] * log_gamma[:, None])
    xi = jnp.exp((idx + 1)[None, :] * log_gamma[:, None])
    gamma_c = jnp.exp(C * log_gamma)
    nc = S // C

    def chunks(t):
        return t.reshape(B, nc, C, H, t.shape[-1]).transpose(1, 0, 3, 2, 4)

    def step(R, inp):
        qc, kc, vc = inp
        inner = jnp.einsum('bhnd,bhmd->bhnm', qc, kc) * dmask
        o = (jnp.einsum('bhnm,bhme->bhne', inner, vc)
             + jnp.einsum('bhnd,bhde->bhne', qc, R) * xi[None, :, :, None])
        R = gamma_c[None, :, None, None] * R + jnp.einsum('bhmd,hm,bhme->bhde', kc, zeta, vc)
        return R, o

    R0 = jnp.zeros((B, H, DK, DV), jnp.float32)
    _, o = lax.scan(step, R0, (chunks(q), chunks(k), chunks(v)))
    o = o.transpose(1, 0, 3, 2, 4).reshape(B, S, H, DV)
    yn = _groupnorm_heads(o, gn_w, gn_b, RET_GN_EPS)
    return (jax.nn.silu(g.astype(jnp.float32)) * yn).astype(h.dtype) @ w_out


def setup_inputs(seed: int = 0) -> dict:
    key = jax.random.key(seed)
    ks = iter(jax.random.split(key, 48))
    f32 = jnp.float32

    def nrm(shape, scale):
        return scale * jax.random.normal(next(ks), shape, f32)

    def uni(shape, lo, hi):
        return jax.random.uniform(next(ks), shape, f32, lo, hi)

    D = D_MODEL
    n_mla, n_rwkv, n_moba, n_ret = [len(range(m, DEPTH, N_MIXERS)) for m in range(N_MIXERS)]
    gate_offset = jnp.zeros((3,), f32).at[2].set(1.0)
    return {
        'x': nrm((BATCH, SEQ, D), 1.0),
        'c': nrm((BATCH, D), 1.0),
        'ada_w': nrm((DEPTH, D, 9 * D), 0.2 * D ** -0.5),
        'ada_b': (nrm((DEPTH, 3, 3, D), 0.02) + gate_offset[None, None, :, None]).reshape(DEPTH, 9 * D),
        'norm_g': 1.0 + nrm((DEPTH, 3, D), 0.05),
        'ffn_w_in': nrm((DEPTH, 2, D, 2 * D_FF), D ** -0.5),
        'ffn_w_out': nrm((DEPTH, 2, D_FF, D), D_FF ** -0.5),
        'final_g': 1.0 + nrm((D,), 0.05),
        'rel_table': nrm((REL_BUCKETS, MOBA_HEADS), 0.5),
        'mla_w_in': nrm((n_mla, D, MLA_Q_RANK + MLA_KV_RANK + MLA_ROPE), D ** -0.5),
        'mla_q_norm': 1.0 + nrm((n_mla, MLA_Q_RANK), 0.05),
        'mla_w_uq': nrm((n_mla, MLA_Q_RANK, MLA_HEADS * (MLA_NOPE + MLA_ROPE)), MLA_Q_RANK ** -0.5),
        'mla_kv_norm': 1.0 + nrm((n_mla, MLA_KV_RANK), 0.05),
        'mla_w_ukv': nrm((n_mla, MLA_KV_RANK, MLA_HEADS * (MLA_NOPE + MLA_V)), MLA_KV_RANK ** -0.5),
        'mla_w_out': nrm((n_mla, MLA_HEADS * MLA_V, D), (MLA_HEADS * MLA_V) ** -0.5),
        'rwkv_mu': uni((n_rwkv, 6, D), 0.0, 1.0),
        'rwkv_w_rkv': nrm((n_rwkv, 3, D, D), D ** -0.5),
        'rwkv_w0': uni((n_rwkv, D), -6.5, -1.5),
        'rwkv_wd1': nrm((n_rwkv, D, RWKV_DECAY_LORA), D ** -0.5),
        'rwkv_wd2': nrm((n_rwkv, RWKV_DECAY_LORA, D), 0.5 * RWKV_DECAY_LORA ** -0.5),
        'rwkv_a0': nrm((n_rwkv, D), 0.1),
        'rwkv_wa1': nrm((n_rwkv, D, RWKV_A_LORA), D ** -0.5),
        'rwkv_wa2': nrm((n_rwkv, RWKV_A_LORA, D), RWKV_A_LORA ** -0.5),
        'rwkv_wg1': nrm((n_rwkv, D, RWKV_GATE_LORA), D ** -0.5),
        'rwkv_wg2': nrm((n_rwkv, RWKV_GATE_LORA, D), RWKV_GATE_LORA ** -0.5),
        'rwkv_k_k': 0.85 + nrm((n_rwkv, D), 0.05),
        'rwkv_k_a': 1.0 + nrm((n_rwkv, D), 0.05),
        'rwkv_r_k': nrm((n_rwkv, RWKV_HEADS, RWKV_HEAD), 0.1),
        'rwkv_gn_w': 1.0 + nrm((n_rwkv, D), 0.05),
        'rwkv_gn_b': nrm((n_rwkv, D), 0.02),
        'rwkv_w_out': nrm((n_rwkv, D, D), D ** -0.5),
        'moba_w_in': nrm((n_moba, D, 3 * D), D ** -0.5),
        'moba_w_out': nrm((n_moba, D, D), D ** -0.5),
        'ret_w_in': nrm((n_ret, D, 6 * D), D ** -0.5),
        'ret_gn_w': 1.0 + nrm((n_ret, 2 * D), 0.05),
        'ret_gn_b': nrm((n_ret, 2 * D), 0.02),
        'ret_w_out': nrm((n_ret, 2 * D, D), (2 * D) ** -0.5),
    }


def reference(x, c, ada_w, ada_b, norm_g, ffn_w_in, ffn_w_out, final_g, rel_table,
              mla_w_in, mla_q_norm, mla_w_uq, mla_kv_norm, mla_w_ukv, mla_w_out,
              rwkv_mu, rwkv_w_rkv, rwkv_w0, rwkv_wd1, rwkv_wd2, rwkv_a0, rwkv_wa1, rwkv_wa2,
              rwkv_wg1, rwkv_wg2, rwkv_k_k, rwkv_k_a, rwkv_r_k, rwkv_gn_w, rwkv_gn_b, rwkv_w_out,
              moba_w_in, moba_w_out, ret_w_in, ret_gn_w, ret_gn_b, ret_w_out):
    B = x.shape[0]
    c_act = jax.nn.silu(c)
    for i in range(DEPTH):
        mods = (c_act @ ada_w[i] + ada_b[i]).reshape(B, 3, 3, D_MODEL)
        h = _modulate(x, norm_g[i, 0], mods[:, 0, 0], mods[:, 0, 1])
        x = x + MACARON_W * mods[:, 0, 2][:, None, :] * _swiglu(h, ffn_w_in[i, 0], ffn_w_out[i, 0])
        h = _modulate(x, norm_g[i, 1], mods[:, 1, 0], mods[:, 1, 1])
        kind, j = i % N_MIXERS, i // N_MIXERS
        if kind == 0:
            y = _mla_mixer(h, mla_w_in[j], mla_q_norm[j], mla_w_uq[j], mla_kv_norm[j], mla_w_ukv[j], mla_w_out[j])
        elif kind == 1:
            y = _rwkv7_mixer(h, rwkv_mu[j], rwkv_w_rkv[j], rwkv_w0[j], rwkv_wd1[j], rwkv_wd2[j], rwkv_a0[j],
                             rwkv_wa1[j], rwkv_wa2[j], rwkv_wg1[j], rwkv_wg2[j], rwkv_k_k[j], rwkv_k_a[j],
                             rwkv_r_k[j], rwkv_gn_w[j], rwkv_gn_b[j], rwkv_w_out[j])
        elif kind == 2:
            y = _moba_mixer(h, moba_w_in[j], rel_table, moba_w_out[j])
        else:
            y = _retnet_mixer(h, ret_w_in[j], ret_gn_w[j], ret_gn_b[j], ret_w_out[j])
        x = x + mods[:, 1, 2][:, None, :] * y
        h = _modulate(x, norm_g[i, 2], mods[:, 2, 0], mods[:, 2, 1])
        x = x + MACARON_W * mods[:, 2, 2][:, None, :] * _swiglu(h, ffn_w_in[i, 1], ffn_w_out[i, 1])
    return _rmsnorm(x, final_g)
```

```python
import functools
import math

import jax
import jax.numpy as jnp
from jax import lax
from jax.experimental import pallas as pl
from jax.experimental.pallas import tpu as pltpu

F32 = jnp.float32
BF16 = jnp.bfloat16
HI = lax.Precision.HIGHEST

LANES = 128
VMEM_LIMIT_BYTES = 56 * 1024 * 1024

D_MODEL = 1024
DEPTH = 4
EPS = 1e-6
MACARON_W = 0.5
D_FF = 2816
ROPE_BASE = 10000.0
REL_BUCKETS = 32
REL_MAX_DIST = 128
MLA_HEADS = 16
MLA_Q_RANK = 384
MLA_KV_RANK = 256
MLA_NOPE = 64
MLA_ROPE = 32
MLA_V = 64
RWKV_HEAD = 64
RWKV_HEADS = D_MODEL // RWKV_HEAD
RWKV_GN_EPS = 64e-5
RWKV_CHUNK = 64
MOBA_HEADS = 16
MOBA_HEAD_DIM = 64
MOBA_BLOCK = 256
MOBA_TOPK = 3
RET_HEADS = 4
RET_DK = 256
RET_DV = 512
RET_CHUNK = 128
RET_GN_EPS = 1e-5
NEG_BIG = -1e30


def _cparams(sem, vmem=VMEM_LIMIT_BYTES):
    return pltpu.CompilerParams(dimension_semantics=sem, vmem_limit_bytes=vmem)


def _resident(block_shape, index_map):
    return pl.BlockSpec(block_shape, index_map, pipeline_mode=pl.Buffered(1))


def _dot(a, b):
    return jnp.dot(a, b, preferred_element_type=F32)


def _dot_hi(a, b):
    return jnp.dot(a, b, preferred_element_type=F32, precision=HI)


def _dot_nt(a, b, precision=None):
    return lax.dot_general(a, b, (((1,), (1,)), ((), ())),
                           preferred_element_type=F32, precision=precision)


def _silu(x):
    return x * jax.nn.sigmoid(x)


def _modnorm(x, g, shift, scale):
    y = x * lax.rsqrt(jnp.mean(x * x, axis=-1, keepdims=True) + EPS)
    return (y * g) * (1 + scale) + shift


def _rms(x, g):
    return (x * lax.rsqrt(jnp.mean(x * x, axis=-1, keepdims=True) + EPS)) * g


def _ada_kernel(c_ref, w_ref, b_ref, o_ref):
    o_ref[...] = _dot_hi(_silu(c_ref[...]), w_ref[...]) + b_ref[...]


def _ada_mods(c, ada_w, ada_b):
    B, D = c.shape
    depth, _, N = ada_w.shape
    rows = 8
    cp = jnp.zeros((rows, D), F32).at[:B].set(c)
    tn = N // 8
    out = pl.pallas_call(
        _ada_kernel,
        grid=(depth, N // tn),
        in_specs=[
            pl.BlockSpec((rows, D), lambda l, n: (0, 0)),
            pl.BlockSpec((None, D, tn), lambda l, n: (l, 0, n)),
            pl.BlockSpec((None, 1, tn), lambda l, n: (l, 0, n)),
        ],
        out_specs=pl.BlockSpec((None, rows, tn), lambda l, n: (l, 0, n)),
        out_shape=jax.ShapeDtypeStruct((depth, rows, N), F32),
        compiler_params=_cparams(("parallel", "parallel")),
        name="ada_mods",
    )(cp, ada_w, ada_b.reshape(depth, 1, N))
    return out[:, :B].reshape(depth, B, 9, D)


def _ffn_kernel(x_ref, m_ref, g_ref, wg_ref, wu_ref, wo_ref, o_ref, *, sub, tf):
    x = x_ref[...]
    shift = m_ref[3 * sub:3 * sub + 1, :]
    scale = m_ref[3 * sub + 1:3 * sub + 2, :]
    gate = m_ref[3 * sub + 2:3 * sub + 3, :]
    h = _modnorm(x, g_ref[...], shift, scale).astype(BF16)
    acc = jnp.zeros(x.shape, F32)
    for j in range(wg_ref.shape[1] // tf):
        gt = _dot(h, wg_ref[:, j * tf:(j + 1) * tf])
        up = _dot(h, wu_ref[:, j * tf:(j + 1) * tf])
        a = (_silu(gt) * up).astype(BF16)
        acc = acc + _dot(a, wo_ref[j * tf:(j + 1) * tf, :])
    o_ref[...] = x + (MACARON_W * gate) * acc


def _ffn(x, mods, g, w_in, w_out, sub, tm=512):
    B, S, D = x.shape
    dff = w_out.shape[0]
    tm = min(tm, S)
    tf = dff // 2
    return pl.pallas_call(
        functools.partial(_ffn_kernel, sub=sub, tf=tf),
        grid=(B, S // tm),
        in_specs=[
            pl.BlockSpec((None, tm, D), lambda b, i: (b, i, 0)),
            pl.BlockSpec((None, 9, D), lambda b, i: (b, 0, 0)),
            pl.BlockSpec((1, D), lambda b, i: (0, 0)),
            _resident((D, dff), lambda b, i: (0, 0)),
            _resident((D, dff), lambda b, i: (0, 1)),
            _resident((dff, D), lambda b, i: (0, 0)),
        ],
        out_specs=pl.BlockSpec((None, tm, D), lambda b, i: (b, i, 0)),
        out_shape=jax.ShapeDtypeStruct((B, S, D), F32),
        compiler_params=_cparams(("parallel", "parallel")),
        name="ffn",
    )(x, mods, g.reshape(1, D), w_in, w_in, w_out)


def _proj_kernel(x_ref, m_ref, g_ref, w_ref, *o_and_scr, sub):
    o_refs, h_scr = o_and_scr[:-1], o_and_scr[-1]

    @pl.when(pl.program_id(2) == 0)
    def _():
        h_scr[...] = _modnorm(x_ref[...], g_ref[...], m_ref[3 * sub:3 * sub + 1, :],
                              m_ref[3 * sub + 1:3 * sub + 2, :]).astype(BF16)

    y = _dot(h_scr[...], w_ref[...])
    for o_ref in o_refs:
        o_ref[...] = y.astype(o_ref.dtype)


def _proj(x, mods, g, w, out_dtypes, sub=1, tm=1024, tn=1024):
    B, S, D = x.shape
    N = w.shape[1]
    tm = min(tm, S)
    tn = tn if N % tn == 0 else N
    outs = pl.pallas_call(
        functools.partial(_proj_kernel, sub=sub),
        grid=(B, S // tm, N // tn),
        in_specs=[
            pl.BlockSpec((None, tm, D), lambda b, i, n: (b, i, 0)),
            pl.BlockSpec((None, 9, D), lambda b, i, n: (b, 0, 0)),
            pl.BlockSpec((1, D), lambda b, i, n: (0, 0)),
            pl.BlockSpec((D, tn), lambda b, i, n: (0, n)),
        ],
        out_specs=[pl.BlockSpec((None, tm, tn), lambda b, i, n: (b, i, n)) for _ in out_dtypes],
        out_shape=[jax.ShapeDtypeStruct((B, S, N), dt) for dt in out_dtypes],
        scratch_shapes=[pltpu.VMEM((tm, D), BF16)],
        compiler_params=_cparams(("parallel", "parallel", "arbitrary")),
        name="modnorm_proj",
    )(x, mods, g.reshape(1, D), w)
    return outs


def _norm_kernel(x_ref, m_ref, g_ref, o_ref, *, sub):
    o_ref[...] = _modnorm(x_ref[...], g_ref[...], m_ref[3 * sub:3 * sub + 1, :],
                          m_ref[3 * sub + 1:3 * sub + 2, :])


def _modnorm_call(x, mods, g, sub=1, tm=1024):
    B, S, D = x.shape
    tm = min(tm, S)
    return pl.pallas_call(
        functools.partial(_norm_kernel, sub=sub),
        grid=(B, S // tm),
        in_specs=[
            pl.BlockSpec((None, tm, D), lambda b, i: (b, i, 0)),
            pl.BlockSpec((None, 9, D), lambda b, i: (b, 0, 0)),
            pl.BlockSpec((1, D), lambda b, i: (0, 0)),
        ],
        out_specs=pl.BlockSpec((None, tm, D), lambda b, i: (b, i, 0)),
        out_shape=jax.ShapeDtypeStruct((B, S, D), F32),
        compiler_params=_cparams(("parallel", "parallel")),
        name="modnorm",
    )(x, mods, g.reshape(1, D))


def _out_kernel(x_ref, y_ref, m_ref, w_ref, o_ref):
    o_ref[...] = x_ref[...] + m_ref[5:6, :] * _dot(y_ref[...], w_ref[...])


def _out_proj(x, y, mods, w, tm=512):
    B, S, D = x.shape
    K = w.shape[0]
    tm = min(tm, S)
    return pl.pallas_call(
        _out_kernel,
        grid=(B, S // tm),
        in_specs=[
            pl.BlockSpec((None, tm, D), lambda b, i: (b, i, 0)),
            pl.BlockSpec((None, tm, K), lambda b, i: (b, i, 0)),
            pl.BlockSpec((None, 9, D), lambda b, i: (b, 0, 0)),
            _resident((K, D), lambda b, i: (0, 0)),
        ],
        out_specs=pl.BlockSpec((None, tm, D), lambda b, i: (b, i, 0)),
        out_shape=jax.ShapeDtypeStruct((B, S, D), F32),
        compiler_params=_cparams(("parallel", "parallel")),
        name="out_proj",
    )(x, y, mods, w)


def _final_kernel(x_ref, g_ref, o_ref):
    o_ref[...] = _rms(x_ref[...], g_ref[...])


def _final_norm(x, g, tm=1024):
    B, S, D = x.shape
    tm = min(tm, S)
    return pl.pallas_call(
        _final_kernel,
        grid=(B, S // tm),
        in_specs=[pl.BlockSpec((None, tm, D), lambda b, i: (b, i, 0)),
                  pl.BlockSpec((1, D), lambda b, i: (0, 0))],
        out_specs=pl.BlockSpec((None, tm, D), lambda b, i: (b, i, 0)),
        out_shape=jax.ShapeDtypeStruct((B, S, D), F32),
        compiler_params=_cparams(("parallel", "parallel")),
        name="final_norm",
    )(x, g.reshape(1, D))


def _rope_tables(seq, dim):
    inv = ROPE_BASE ** (-jnp.arange(0, dim, 2, dtype=F32) / dim)
    ang = jnp.arange(seq, dtype=F32)[:, None] * inv[None, :]
    return jnp.cos(ang), jnp.sin(ang)


def _mla_prep_kernel(p_ref, qn_ref, kvn_ref, wq_ref, wqs_ref, wk_ref, wv_ref, c_ref, s_ref,
                     q_ref, k_ref, v_ref):
    p = p_ref[...]
    cq = _rms(p[:, :MLA_Q_RANK], qn_ref[...]).astype(BF16)
    ckv = _rms(p[:, MLA_Q_RANK:MLA_Q_RANK + MLA_KV_RANK], kvn_ref[...]).astype(BF16)
    o = MLA_Q_RANK + MLA_KV_RANK
    cos = c_ref[...]
    sin = s_ref[...]
    k_rope = p[:, o:o + LANES] * cos + p[:, o + LANES:o + 2 * LANES] * sin
    q1 = _dot(cq, wq_ref[...])
    q2 = _dot(cq, wqs_ref[...])
    kn = _dot(ckv, wk_ref[...])
    scale = (MLA_NOPE + MLA_ROPE) ** -0.5
    for h in range(MLA_HEADS):
        sl = slice(h * LANES, (h + 1) * LANES)
        q_ref[:, sl] = ((q1[:, sl] * cos + q2[:, sl] * sin) * scale).astype(BF16)
        k_ref[:, sl] = (kn[:, sl] + k_rope).astype(BF16)
    v_ref[...] = _dot(ckv, wv_ref[...]).astype(BF16)


def _flash_update(s, vb, m_scr, l_scr, acc_scr, h):
    m_old = m_scr[h]
    m_new = jnp.maximum(m_old, jnp.max(s, axis=-1, keepdims=True))
    alpha = jnp.exp(m_old - m_new)
    p = jnp.exp(s - m_new)
    l_scr[h] = alpha * l_scr[h] + jnp.sum(p, axis=-1, keepdims=True)
    acc_scr[h] = alpha * acc_scr[h] + _dot(p.astype(BF16), vb)
    m_scr[h] = m_new


def _flash_init(m_scr, l_scr, acc_scr):
    m_scr[...] = jnp.full(m_scr.shape, -jnp.inf, F32)
    l_scr[...] = jnp.zeros(l_scr.shape, F32)
    acc_scr[...] = jnp.zeros(acc_scr.shape, F32)


def _flash_finish(o_ref, l_scr, acc_scr):
    lane = lax.broadcasted_iota(jnp.int32, acc_scr.shape[1:], 1)
    o0 = acc_scr[0] / l_scr[0]
    o1 = acc_scr[1] / l_scr[1]
    o_ref[...] = jnp.where(lane < 64, o0, o1).astype(o_ref.dtype)


def _mla_attn_kernel(q_ref, k_ref, v_ref, o_ref, m_scr, l_scr, acc_scr, *, t):
    i = pl.program_id(2)
    _flash_init(m_scr, l_scr, acc_scr)
    q = q_ref[...]

    def block(j, diagonal):
        start = pl.multiple_of(j * t, t)
        kb = k_ref[pl.ds(start, t), :]
        vb = v_ref[pl.ds(start, t), :]
        for h in range(2):
            s = _dot_nt(q[:, h * LANES:(h + 1) * LANES], kb[:, h * LANES:(h + 1) * LANES])
            if diagonal:
                row = lax.broadcasted_iota(jnp.int32, s.shape, 0)
                col = lax.broadcasted_iota(jnp.int32, s.shape, 1)
                s = jnp.where(col <= row, s, -jnp.inf)
            _flash_update(s, vb, m_scr, l_scr, acc_scr, h)

    def body(j, carry):
        block(j, False)
        return carry

    lax.fori_loop(0, i, body, 0)
    block(i, True)
    _flash_finish(o_ref, l_scr, acc_scr)


def _mla_weights(w_in, w_uq, w_ukv):
    H, dn, dr, dv = MLA_HEADS, MLA_NOPE, MLA_ROPE, MLA_V
    D = w_in.shape[0]
    o = MLA_Q_RANK + MLA_KV_RANK
    kr = w_in[:, o:]
    z64 = jnp.zeros((D, dn), F32)
    z32 = jnp.zeros((D, LANES - dn - dr), F32)
    half = dr // 2
    w_in_p = jnp.concatenate(
        [w_in[:, :o], z64, kr, z32, z64, -kr[:, half:], kr[:, :half], z32], axis=1)
    wq = w_uq.reshape(MLA_Q_RANK, H, dn + dr)
    zq = jnp.zeros((MLA_Q_RANK, H, LANES - dn - dr), F32)
    wq_p = jnp.concatenate([wq, zq], axis=2).reshape(MLA_Q_RANK, H * LANES)
    wqs_p = jnp.concatenate(
        [jnp.zeros((MLA_Q_RANK, H, dn), F32), -wq[:, :, dn + half:], wq[:, :, dn:dn + half], zq],
        axis=2).reshape(MLA_Q_RANK, H * LANES)
    wkv = w_ukv.reshape(MLA_KV_RANK, H, dn + dv)
    wk_p = jnp.concatenate(
        [wkv[:, :, :dn], jnp.zeros((MLA_KV_RANK, H, LANES - dn), F32)], axis=2).reshape(MLA_KV_RANK, H * LANES)
    wv_p = wkv[:, :, dn:].reshape(MLA_KV_RANK, H * dv)
    return (w_in_p.astype(BF16), wq_p.astype(BF16), wqs_p.astype(BF16),
            wk_p.astype(BF16), wv_p.astype(BF16))


def _mla_mixer(x, mods, g, w_in, q_norm, w_uq, kv_norm, w_ukv, w_out, tm=512, t=256):
    B, S, D = x.shape
    H = MLA_HEADS
    tm = min(tm, S)
    t = min(t, S)
    w_in_p, wq_p, wqs_p, wk_p, wv_p = _mla_weights(w_in, w_uq, w_ukv)
    (p,) = _proj(x, mods, g, w_in_p, [F32])
    cos, sin = _rope_tables(S, MLA_ROPE)
    one = jnp.ones((S, MLA_NOPE), F32)
    pad = LANES - MLA_NOPE - MLA_ROPE
    tab_c = jnp.concatenate([one, cos, cos, jnp.ones((S, pad), F32)], axis=1)
    tab_s = jnp.concatenate([0 * one, sin, sin, jnp.zeros((S, pad), F32)], axis=1)
    NP = p.shape[-1]
    q, k, v = pl.pallas_call(
        _mla_prep_kernel,
        grid=(B, S // tm),
        in_specs=[
            pl.BlockSpec((None, tm, NP), lambda b, i: (b, i, 0)),
            pl.BlockSpec((1, MLA_Q_RANK), lambda b, i: (0, 0)),
            pl.BlockSpec((1, MLA_KV_RANK), lambda b, i: (0, 0)),
            _resident(wq_p.shape, lambda b, i: (0, 0)),
            _resident(wqs_p.shape, lambda b, i: (0, 0)),
            _resident(wk_p.shape, lambda b, i: (0, 0)),
            _resident(wv_p.shape, lambda b, i: (0, 0)),
            pl.BlockSpec((tm, LANES), lambda b, i: (i, 0)),
            pl.BlockSpec((tm, LANES), lambda b, i: (i, 0)),
        ],
        out_specs=[
            pl.BlockSpec((None, tm, H * LANES), lambda b, i: (b, i, 0)),
            pl.BlockSpec((None, tm, H * LANES), lambda b, i: (b, i, 0)),
            pl.BlockSpec((None, tm, H * MLA_V), lambda b, i: (b, i, 0)),
        ],
        out_shape=[
            jax.ShapeDtypeStruct((B, S, H * LANES), BF16),
            jax.ShapeDtypeStruct((B, S, H * LANES), BF16),
            jax.ShapeDtypeStruct((B, S, H * MLA_V), BF16),
        ],
        compiler_params=_cparams(("parallel", "parallel")),
        name="mla_prep",
    )(p, q_norm.reshape(1, -1), kv_norm.reshape(1, -1), wq_p, wqs_p, wk_p, wv_p, tab_c, tab_s)
    o = pl.pallas_call(
        functools.partial(_mla_attn_kernel, t=t),
        grid=(B, H // 2, S // t),
        in_specs=[
            pl.BlockSpec((None, t, 2 * LANES), lambda b, hp, i: (b, i, hp)),
            pl.BlockSpec((None, S, 2 * LANES), lambda b, hp, i: (b, 0, hp)),
            pl.BlockSpec((None, S, LANES), lambda b, hp, i: (b, 0, hp)),
        ],
        out_specs=pl.BlockSpec((None, t, LANES), lambda b, hp, i: (b, i, hp)),
        out_shape=jax.ShapeDtypeStruct((B, S, H * MLA_V), BF16),
        scratch_shapes=[pltpu.VMEM((2, t, 1), F32), pltpu.VMEM((2, t, 1), F32),
                        pltpu.VMEM((2, t, LANES), F32)],
        compiler_params=_cparams(("parallel", "parallel", "arbitrary")),
        name="mla_attn",
    )(q, k, v)
    return _out_proj(x, o, mods, w_out.astype(BF16))


def _moba_kmean_kernel(k_ref, o_ref):
    o_ref[...] = jnp.mean(k_ref[...], axis=0, keepdims=True)


def _moba_select_kernel(q_ref, km_ref, o_ref, *, nb):
    H, dh, tq = MOBA_HEADS, MOBA_HEAD_DIM, MOBA_BLOCK
    own = pl.program_id(1)
    km = km_ref[...]
    kbd = jnp.concatenate([km] * H, axis=0)
    assert nb & (nb - 1) == 0 and dh & (dh - 1) == 0
    row_head = lax.shift_right_logical(lax.broadcasted_iota(jnp.int32, kbd.shape, 0), nb.bit_length() - 1)
    lane_head = lax.shift_right_logical(lax.broadcasted_iota(jnp.int32, kbd.shape, 1), dh.bit_length() - 1)
    kbd = jnp.where(row_head == lane_head, kbd, 0.0)
    gate = _dot_nt(kbd, q_ref[...], precision=HI)
    g = gate.reshape(H, nb, tq)
    n_iota = lax.broadcasted_iota(jnp.int32, g.shape, 1)
    g = jnp.where(n_iota < own, g, -jnp.inf)
    sel = jnp.zeros(g.shape, jnp.bool_)
    for _ in range(min(MOBA_TOPK, nb)):
        m = jnp.max(g, axis=1, keepdims=True)
        first = jnp.min(jnp.where(g == m, n_iota, nb), axis=1, keepdims=True)
        pick = n_iota == first
        sel = sel | (pick & (m > -jnp.inf))
        g = jnp.where(pick, -jnp.inf, g)
    bias = jnp.where(sel, 0.0, NEG_BIG)
    if nb < LANES:
        bias = jnp.concatenate([bias, jnp.zeros((H, LANES - nb, tq), F32)], axis=1)
    o_ref[...] = bias.reshape(H * LANES, tq).T.astype(o_ref.dtype)


def _moba_bias_kernel(tbl_ref, o_ref):
    h = pl.program_id(0)
    blk = MOBA_BLOCK
    row = lax.broadcasted_iota(jnp.int32, (blk, blk), 0)
    col = lax.broadcasted_iota(jnp.int32, (blk, blk), 1)
    max_exact = REL_BUCKETS // 2
    for t in range(2):
        n = jnp.maximum(row - col + blk * t, 0)
        nf = jnp.maximum(n, max_exact).astype(F32)
        large = max_exact + (jnp.log(nf / max_exact) / math.log(REL_MAX_DIST / max_exact)
                             * (REL_BUCKETS - max_exact)).astype(jnp.int32)
        large = jnp.minimum(large, REL_BUCKETS - 1)
        bucket = jnp.where(n < max_exact, n, large)
        acc = jnp.zeros((blk, blk), F32)
        for b in range(REL_BUCKETS):
            acc = jnp.where(bucket == b, tbl_ref[b, h], acc)
        o_ref[t] = acc


def _moba_attn_kernel(tbl_ref, q_ref, k_ref, v_ref, sel_ref, bias_ref, o_ref,
                      m_scr, l_scr, acc_scr):
    hp = pl.program_id(1)
    i = pl.program_id(2)
    t = MOBA_BLOCK
    _flash_init(m_scr, l_scr, acc_scr)
    q = q_ref[...]
    lane = lax.broadcasted_iota(jnp.int32, q.shape, 1)
    scale = MOBA_HEAD_DIM ** -0.5
    qh = [jnp.where((lane < 64) == (h == 0), q, 0) * scale for h in range(2)]
    qa = [jnp.concatenate([qh[h], sel_ref[:, h * LANES:(h + 1) * LANES]], axis=1) for h in range(2)]

    def past_block(j, bias_fn):
        start = pl.multiple_of(j * t, t)
        kb = k_ref[pl.ds(start, t), :]
        vb = v_ref[pl.ds(start, t), :]
        onehot = jnp.where(lane == j, 1.0, 0.0).astype(BF16)
        ka = jnp.concatenate([kb, onehot], axis=1)
        for h in range(2):
            s = _dot_nt(qa[h], ka) + bias_fn(h)
            _flash_update(s, vb, m_scr, l_scr, acc_scr, h)

    def far_body(j, carry):
        past_block(j, lambda h: tbl_ref[REL_BUCKETS - 1, 2 * hp + h])
        return carry

    lax.fori_loop(0, jnp.maximum(i - 1, 0), far_body, 0)

    @pl.when(i > 0)
    def _():
        past_block(i - 1, lambda h: bias_ref[h, 1])

    start = pl.multiple_of(i * t, t)
    kb = k_ref[pl.ds(start, t), :]
    vb = v_ref[pl.ds(start, t), :]
    row = lax.broadcasted_iota(jnp.int32, (t, t), 0)
    col = lax.broadcasted_iota(jnp.int32, (t, t), 1)
    for h in range(2):
        s = _dot_nt(qh[h], kb) + bias_ref[h, 0]
        s = jnp.where(col <= row, s, -jnp.inf)
        _flash_update(s, vb, m_scr, l_scr, acc_scr, h)
    _flash_finish(o_ref, l_scr, acc_scr)


def _moba_mixer(x, mods, g, w_in, rel_table, w_out):
    B, S, D = x.shape
    H, t = MOBA_HEADS, MOBA_BLOCK
    nb = S // t
    qkv, qkv16 = _proj(x, mods, g, w_in.astype(BF16), [F32, BF16])
    kmean = pl.pallas_call(
        _moba_kmean_kernel,
        grid=(B, nb),
        in_specs=[pl.BlockSpec((None, t, D), lambda b, n: (b, n, 1))],
        out_specs=pl.BlockSpec((None, None, 1, D), lambda b, n: (b, n, 0, 0)),
        out_shape=jax.ShapeDtypeStruct((B, nb, 1, D), F32),
        compiler_params=_cparams(("parallel", "parallel")),
        name="moba_kmean",
    )(qkv).reshape(B, nb, D)
    sel = pl.pallas_call(
        functools.partial(_moba_select_kernel, nb=nb),
        grid=(B, nb),
        in_specs=[pl.BlockSpec((None, t, D), lambda b, i: (b, i, 0)),
                  pl.BlockSpec((None, nb, D), lambda b, i: (b, 0, 0))],
        out_specs=pl.BlockSpec((None, t, H * LANES), lambda b, i: (b, i, 0)),
        out_shape=jax.ShapeDtypeStruct((B, S, H * LANES), BF16),
        compiler_params=_cparams(("parallel", "parallel")),
        name="moba_select",
    )(qkv, kmean)
    bias = pl.pallas_call(
        _moba_bias_kernel,
        grid=(H,),
        in_specs=[pl.BlockSpec(memory_space=pltpu.SMEM)],
        out_specs=pl.BlockSpec((None, 2, t, t), lambda h: (h, 0, 0, 0)),
        out_shape=jax.ShapeDtypeStruct((H, 2, t, t), F32),
        compiler_params=_cparams(("parallel",)),
        name="moba_bias",
    )(rel_table)
    nhp = H // 2
    o = pl.pallas_call(
        _moba_attn_kernel,
        grid=(B, nhp, nb),
        in_specs=[
            pl.BlockSpec(memory_space=pltpu.SMEM),
            pl.BlockSpec((None, t, LANES), lambda b, hp, i: (b, i, hp)),
            pl.BlockSpec((None, S, LANES), lambda b, hp, i: (b, 0, nhp + hp)),
            pl.BlockSpec((None, S, LANES), lambda b, hp, i: (b, 0, 2 * nhp + hp)),
            pl.BlockSpec((None, t, 2 * LANES), lambda b, hp, i: (b, i, hp)),
            pl.BlockSpec((2, 2, t, t), lambda b, hp, i: (hp, 0, 0, 0)),
        ],
        out_specs=pl.BlockSpec((None, t, LANES), lambda b, hp, i: (b, i, hp)),
        out_shape=jax.ShapeDtypeStruct((B, S, D), BF16),
        scratch_shapes=[pltpu.VMEM((2, t, 1), F32), pltpu.VMEM((2, t, 1), F32),
                        pltpu.VMEM((2, t, LANES), F32)],
        compiler_params=_cparams(("parallel", "parallel", "arbitrary")),
        name="moba_attn",
    )(rel_table, qkv16, qkv16, qkv16, sel, bias)
    return _out_proj(x, o, mods, w_out.astype(BF16))


def _ret_kernel(gam_ref, q_ref, k_ref, v_ref, g_ref, cos_ref, sin_ref, dm_ref, zeta_ref, xi_ref,
                gnw_ref, gnb_ref, o_ref, r_scr):
    h = pl.program_id(1)

    @pl.when(pl.program_id(2) == 0)
    def _():
        r_scr[...] = jnp.zeros(r_scr.shape, F32)

    cos = cos_ref[...]
    sin = sin_ref[...]
    half = RET_DK // 2

    def rope(t):
        t1, t2 = t[:, :half], t[:, half:]
        return jnp.concatenate([t1 * cos - t2 * sin, t1 * sin + t2 * cos], axis=1)

    q = rope(q_ref[...])
    k = rope(k_ref[...] * RET_DK ** -0.5)
    v = v_ref[...]
    R = r_scr[...]
    inner = _dot_nt(q, k, precision=HI) * dm_ref[...]
    o = _dot_hi(inner, v) + _dot_hi(q, R) * xi_ref[...]
    r_scr[...] = gam_ref[h] * R + _dot_hi((k * zeta_ref[...]).T, v)
    mu = jnp.mean(o, axis=-1, keepdims=True)
    d = o - mu
    var = jnp.mean(d * d, axis=-1, keepdims=True)
    yn = (d * lax.rsqrt(var + RET_GN_EPS)) * gnw_ref[...] + gnb_ref[...]
    o_ref[...] = (_silu(g_ref[...]) * yn).astype(o_ref.dtype)


def _retnet_mixer(x, mods, g, w_in, gn_w, gn_b, w_out):
    B, S, D = x.shape
    H, DK, DV, C = RET_HEADS, RET_DK, RET_DV, RET_CHUNK
    C = min(C, S)
    (p,) = _proj(x, mods, g, w_in.astype(BF16), [F32])
    cos, sin = _rope_tables(S, DK)
    log_gamma = jnp.log1p(-jnp.exp2(-5.0 - jnp.arange(H, dtype=F32)))
    idx = jnp.arange(C, dtype=F32)
    diff = idx[:, None] - idx[None, :]
    dmask = jnp.where(diff >= 0, jnp.exp(jnp.maximum(diff, 0.0)[None] * log_gamma[:, None, None]), 0.0)
    zeta = jnp.exp((C - 1 - idx)[None, :] * log_gamma[:, None])[:, :, None]
    xi = jnp.exp((idx + 1)[None, :] * log_gamma[:, None])[:, :, None]
    gamma_c = jnp.exp(C * log_gamma)
    kq, kk, kv, kg = 0, D // DK, 2 * D // DV, 4 * D // DV
    y = pl.pallas_call(
        _ret_kernel,
        grid=(B, H, S // C),
        in_specs=[
            pl.BlockSpec(memory_space=pltpu.SMEM),
            pl.BlockSpec((None, C, DK), lambda b, h, c: (b, c, kq + h)),
            pl.BlockSpec((None, C, DK), lambda b, h, c: (b, c, kk + h)),
            pl.BlockSpec((None, C, DV), lambda b, h, c: (b, c, kv + h)),
            pl.BlockSpec((None, C, DV), lambda b, h, c: (b, c, kg + h)),
            pl.BlockSpec((C, DK // 2), lambda b, h, c: (c, 0)),
            pl.BlockSpec((C, DK // 2), lambda b, h, c: (c, 0)),
            pl.BlockSpec((None, C, C), lambda b, h, c: (h, 0, 0)),
            pl.BlockSpec((None, C, 1), lambda b, h, c: (h, 0, 0)),
            pl.BlockSpec((None, C, 1), lambda b, h, c: (h, 0, 0)),
            pl.BlockSpec((1, DV), lambda b, h, c: (0, h)),
            pl.BlockSpec((1, DV), lambda b, h, c: (0, h)),
        ],
        out_specs=pl.BlockSpec((None, C, DV), lambda b, h, c: (b, c, h)),
        out_shape=jax.ShapeDtypeStruct((B, S, H * DV), BF16),
        scratch_shapes=[pltpu.VMEM((DK, DV), F32)],
        compiler_params=_cparams(("parallel", "parallel", "arbitrary")),
        name="retention",
    )(gamma_c, p, p, p, p, cos, sin, dmask, zeta, xi, gn_w.reshape(1, -1), gn_b.reshape(1, -1))
    return _out_proj(x, y, mods, w_out.astype(BF16))


def _head_sum(t, gs_ref, gst_ref):
    return _dot_hi(_dot_hi(t, gs_ref[...]), gst_ref[...])


def _rwkv_prep_kernel(h_ref, hp_ref, mu_ref, wrkv_ref, w0_ref, wd1_ref, wd2_ref, a0_ref, wa1_ref,
                      wa2_ref, wg1_ref, wg2_ref, kk_ref, ka_ref, rk_ref, gs_ref, gst_ref,
                      r_o, lw_o, k_o, v_o, kk_o, a_o, g_o, bonus_o):
    i = pl.program_id(1)
    h = h_ref[...]
    tm = h.shape[0]
    prev_tail = jnp.where(i > 0, hp_ref[7:8, :], 0.0)
    rolled = pltpu.roll(h, 1, axis=0)
    row = lax.broadcasted_iota(jnp.int32, h.shape, 0)
    xx = jnp.where(row == 0, prev_tail, rolled) - h
    xs = [h + xx * mu_ref[n:n + 1, :] for n in range(6)]
    r = _dot(xs[0].astype(BF16), wrkv_ref[0])
    k = _dot(xs[1].astype(BF16), wrkv_ref[1])
    v = _dot(xs[2].astype(BF16), wrkv_ref[2])
    z = -(w0_ref[...] + _dot_hi(jnp.tanh(_dot_hi(xs[3], wd1_ref[...])), wd2_ref[...]))
    softplus = jnp.maximum(z, 0.0) + jnp.log(1.0 + jnp.exp(-jnp.abs(z)))
    w = -softplus - 0.5
    lw_o[...] = -jnp.exp(w)
    a = jax.nn.sigmoid(a0_ref[...] + _dot_hi(_dot_hi(xs[4], wa1_ref[...]), wa2_ref[...]))
    g_o[...] = _dot_hi(jax.nn.sigmoid(_dot_hi(xs[5], wg1_ref[...])), wg2_ref[...])
    kk = k * kk_ref[...]
    norm = jnp.sqrt(_head_sum(kk * kk, gs_ref, gst_ref))
    kk_o[...] = kk / jnp.maximum(norm, 1e-12)
    k_mod = k * (1 + (a - 1) * ka_ref[...])
    r_o[...] = r
    k_o[...] = k_mod
    v_o[...] = v
    a_o[...] = a
    bonus_o[...] = _head_sum(r * k_mod * rk_ref[...], gs_ref, gst_ref) * v


def _rwkv_scan_kernel(r_ref, lw_ref, k_ref, v_ref, kk_ref, a_ref, y_ref, st_scr, *, pairs):
    C = RWKV_CHUNK
    P = 2 * C

    @pl.when(pl.program_id(2) == 0)
    def _():
        st_scr[...] = jnp.zeros(st_scr.shape, F32)

    tri = (lax.broadcasted_iota(jnp.int32, (C, C), 0) >= lax.broadcasted_iota(jnp.int32, (C, C), 1))
    tri = jnp.where(tri, 1.0, 0.0)
    lane = lax.broadcasted_iota(jnp.int32, (C, LANES), 1)
    first = lane < RWKV_HEAD
    rt = lax.broadcasted_iota(jnp.int32, (P, P), 0) & (C - 1)
    ct = lax.broadcasted_iota(jnp.int32, (P, P), 1) & (C - 1)
    strict = rt > ct
    incl = rt >= ct
    eye = jnp.where(lax.broadcasted_iota(jnp.int32, (P, P), 0) == lax.broadcasted_iota(jnp.int32, (P, P), 1),
                    1.0, 0.0)

    def stack(t):
        return jnp.concatenate([jnp.where(first, t, 0.0), jnp.where(first, 0.0, t)], axis=0)

    for p in range(pairs):
        sl = slice(p * LANES, (p + 1) * LANES)
        lw = lw_ref[:, sl]
        kk = kk_ref[:, sl]
        cum = _dot_hi(tri, lw)
        p_inc = jnp.exp(cum)
        p_exc = jnp.exp(cum - lw)
        p_inv = jnp.exp(-cum)
        a_s = stack(-kk * p_exc)
        b_s = stack(kk * a_ref[:, sl] * p_inv)
        k_s = stack(k_ref[:, sl] * p_inv)
        r_s = stack(r_ref[:, sl] * p_inc)
        v_s = stack(v_ref[:, sl])
        ar = jnp.concatenate([a_s, r_s], axis=0)
        bk = jnp.concatenate([b_s, k_s], axis=0)
        gram = _dot_nt(ar, bk, precision=HI)
        l_ab = jnp.where(strict, gram[:P, :P], 0.0)
        l_ak = jnp.where(strict, gram[:P, P:], 0.0)
        a_rb = jnp.where(incl, gram[P:, :P], 0.0)
        a_rk = jnp.where(incl, gram[P:, P:], 0.0)
        t_inv = eye + l_ab
        lp = l_ab
        for _ in range(int(math.log2(C)) - 1):
            lp = _dot_hi(lp, lp)
            t_inv = t_inv + _dot_hi(t_inv, lp)
        st = st_scr[p]
        arh = _dot_nt(ar, st, precision=HI)
        u = _dot_hi(t_inv, arh[:P] + _dot_hi(l_ak, v_s))
        o_s = arh[P:] + _dot_hi(a_rb, u) + _dot_hi(a_rk, v_s)
        y_ref[:, sl] = o_s[:C] + o_s[C:]
        uv = jnp.concatenate([u, v_s], axis=0)
        st_scr[p] = (st + _dot_hi(uv.T, bk)) * p_inc[C - 1:C, :]


def _rwkv_post_kernel(x_ref, y_ref, bonus_ref, g_ref, m_ref, gnw_ref, gnb_ref, gs_ref, gst_ref,
                      w_ref, o_ref):
    y = y_ref[...]
    inv_n = 1.0 / RWKV_HEAD
    mu = _head_sum(y, gs_ref, gst_ref) * inv_n
    d = y - mu
    var = _head_sum(d * d, gs_ref, gst_ref) * inv_n
    yn = (d * lax.rsqrt(var + RWKV_GN_EPS)) * gnw_ref[...] + gnb_ref[...]
    z = ((yn + bonus_ref[...]) * g_ref[...]).astype(BF16)
    o_ref[...] = x_ref[...] + m_ref[5:6, :] * _dot(z, w_ref[...])


def _rwkv7_mixer(x, mods, g, mu, w_rkv, w0, wd1, wd2, a0, wa1, wa2, wg1, wg2, k_k, k_a, r_k,
                 gn_w, gn_b, w_out, tm=256, pairs=4):
    B, S, D = x.shape
    H, N, C = RWKV_HEADS, RWKV_HEAD, RWKV_CHUNK
    tm = min(tm, S)
    h = _modnorm_call(x, mods, g)
    gs = (jnp.arange(D)[:, None] // N == jnp.arange(LANES)[None, :]).astype(F32)
    H = LANES

    def pad_rank(w1, w2):
        r = w1.shape[1]
        rp = -(-r // LANES) * LANES
        return jnp.pad(w1, ((0, 0), (0, rp - r))), jnp.pad(w2, ((0, rp - r), (0, 0)))

    wd1, wd2 = pad_rank(wd1, wd2)
    wa1, wa2 = pad_rank(wa1, wa2)
    wg1, wg2 = pad_rank(wg1, wg2)
    row = lambda t: t.reshape(1, D)
    full = lambda shape: _resident(shape, lambda b, i: (0,) * len(shape))
    tile = pl.BlockSpec((None, tm, D), lambda b, i: (b, i, 0))
    outs = pl.pallas_call(
        _rwkv_prep_kernel,
        grid=(B, S // tm),
        in_specs=[
            tile,
            pl.BlockSpec((None, 8, D), lambda b, i: (b, jnp.maximum(i * (tm // 8) - 1, 0), 0)),
            full((6, D)), full((3, D, D)), full((1, D)), full(wd1.shape), full(wd2.shape),
            full((1, D)), full(wa1.shape), full(wa2.shape), full(wg1.shape), full(wg2.shape),
            full((1, D)), full((1, D)), full((1, D)), full((D, H)), full((H, D)),
        ],
        out_specs=[tile] * 8,
        out_shape=[jax.ShapeDtypeStruct((B, S, D), F32)] * 8,
        compiler_params=_cparams(("parallel", "parallel")),
        name="rwkv_prep",
    )(h, h, mu, w_rkv.astype(BF16), row(w0), wd1, wd2, row(a0), wa1, wa2, wg1, wg2,
      row(k_k), row(k_a), row(r_k), gs, gs.T)
    r, lw, k, v, kk, a, gg, bonus = outs
    width = pairs * LANES
    blk = pl.BlockSpec((None, C, width), lambda b, hg, c: (b, c, hg))
    y = pl.pallas_call(
        functools.partial(_rwkv_scan_kernel, pairs=pairs),
        grid=(B, D // width, S // C),
        in_specs=[blk] * 6,
        out_specs=blk,
        out_shape=jax.ShapeDtypeStruct((B, S, D), F32),
        scratch_shapes=[pltpu.VMEM((pairs, LANES, LANES), F32)],
        compiler_params=_cparams(("parallel", "parallel", "arbitrary")),
        name="rwkv_scan",
    )(r, lw, k, v, kk, a)
    tp = min(512, S)
    tile_p = pl.BlockSpec((None, tp, D), lambda b, i: (b, i, 0))
    return pl.pallas_call(
        _rwkv_post_kernel,
        grid=(B, S // tp),
        in_specs=[tile_p, tile_p, tile_p, tile_p,
                  pl.BlockSpec((None, 9, D), lambda b, i: (b, 0, 0)),
                  full((1, D)), full((1, D)), full((D, H)), full((H, D)), full((D, D))],
        out_specs=tile_p,
        out_shape=jax.ShapeDtypeStruct((B, S, D), F32),
        compiler_params=_cparams(("parallel", "parallel")),
        name="rwkv_post",
    )(x, y, bonus, gg, mods, row(gn_w), row(gn_b), gs, gs.T, w_out.astype(BF16))


def kernel(x, c, ada_w, ada_b, norm_g, ffn_w_in, ffn_w_out, final_g, rel_table, mla_w_in, mla_q_norm, mla_w_uq, mla_kv_norm, mla_w_ukv, mla_w_out, rwkv_mu, rwkv_w_rkv, rwkv_w0, rwkv_wd1, rwkv_wd2, rwkv_a0, rwkv_wa1, rwkv_wa2, rwkv_wg1, rwkv_wg2, rwkv_k_k, rwkv_k_a, rwkv_r_k, rwkv_gn_w, rwkv_gn_b, rwkv_w_out, moba_w_in, moba_w_out, ret_w_in, ret_gn_w, ret_gn_b, ret_w_out):
    depth = ada_w.shape[0]
    mods_all = _ada_mods(c, ada_w, ada_b)
    ffn_in = ffn_w_in.astype(BF16)
    ffn_out = ffn_w_out.astype(BF16)
    for i in range(depth):
        mods = mods_all[i]
        x = _ffn(x, mods, norm_g[i, 0], ffn_in[i, 0], ffn_out[i, 0], sub=0)
        kind, j = i % 4, i // 4
        g = norm_g[i, 1]
        if kind == 0:
            x = _mla_mixer(x, mods, g, mla_w_in[j], mla_q_norm[j], mla_w_uq[j], mla_kv_norm[j],
                           mla_w_ukv[j], mla_w_out[j])
        elif kind == 1:
            x = _rwkv7_mixer(x, mods, g, rwkv_mu[j], rwkv_w_rkv[j], rwkv_w0[j], rwkv_wd1[j],
                             rwkv_wd2[j], rwkv_a0[j], rwkv_wa1[j], rwkv_wa2[j], rwkv_wg1[j],
                             rwkv_wg2[j], rwkv_k_k[j], rwkv_k_a[j], rwkv_r_k[j].reshape(-1),
                             rwkv_gn_w[j], rwkv_gn_b[j], rwkv_w_out[j])
        elif kind == 2:
            x = _moba_mixer(x, mods, g, moba_w_in[j], rel_table, moba_w_out[j])
        else:
            x = _retnet_mixer(x, mods, g, ret_w_in[j], ret_gn_w[j], ret_gn_b[j], ret_w_out[j])
        x = _ffn(x, mods, norm_g[i, 2], ffn_in[i, 1], ffn_out[i, 1], sub=2)
    return _final_norm(x, final_g)
```

```python
import functools
import math

import jax
import jax.numpy as jnp
from jax import lax
from jax.experimental import pallas as pl
from jax.experimental.pallas import tpu as pltpu

F32 = jnp.float32
BF16 = jnp.bfloat16
HI = lax.Precision.HIGHEST

LANES = 128
VMEM_LIMIT_BYTES = 56 * 1024 * 1024

D_MODEL = 1024
DEPTH = 4
EPS = 1e-6
MACARON_W = 0.5
D_FF = 2816
ROPE_BASE = 10000.0
REL_BUCKETS = 32
REL_MAX_DIST = 128
MLA_HEADS = 16
MLA_Q_RANK = 384
MLA_KV_RANK = 256
MLA_NOPE = 64
MLA_ROPE = 32
MLA_V = 64
RWKV_HEAD = 64
RWKV_HEADS = D_MODEL // RWKV_HEAD
RWKV_GN_EPS = 64e-5
RWKV_CHUNK = 64
MOBA_HEADS = 16
MOBA_HEAD_DIM = 64
MOBA_BLOCK = 256
MOBA_TOPK = 3
RET_HEADS = 4
RET_DK = 256
RET_DV = 512
RET_CHUNK = 128
RET_GN_EPS = 1e-5
NEG_BIG = -1e30


def _cparams(sem, vmem=VMEM_LIMIT_BYTES):
    return pltpu.CompilerParams(dimension_semantics=sem, vmem_limit_bytes=vmem)


def _resident(block_shape, index_map):
    return pl.BlockSpec(block_shape, index_map, pipeline_mode=pl.Buffered(1))


def _dot(a, b):
    return jnp.dot(a, b, preferred_element_type=F32)


def _dot_hi(a, b):
    return jnp.dot(a, b, preferred_element_type=F32, precision=HI)


def _split(x):
    hi = x.astype(BF16)
    return hi, (x - hi.astype(F32)).astype(BF16)


def _dot3(a, b):
    return _dot(a[0], b[0]) + (_dot(a[0], b[1]) + _dot(a[1], b[0]))


def _dot3_nt(a, b):
    return _dot_nt(a[0], b[0]) + (_dot_nt(a[0], b[1]) + _dot_nt(a[1], b[0]))


def _dot_nt(a, b, precision=None):
    return lax.dot_general(a, b, (((1,), (1,)), ((), ())),
                           preferred_element_type=F32, precision=precision)


def _silu(x):
    return x * jax.nn.sigmoid(x)


def _modnorm(x, g, shift, scale):
    y = x * lax.rsqrt(jnp.mean(x * x, axis=-1, keepdims=True) + EPS)
    return (y * g) * (1 + scale) + shift


def _rms(x, g):
    return (x * lax.rsqrt(jnp.mean(x * x, axis=-1, keepdims=True) + EPS)) * g


def _ada_kernel(c_ref, w_ref, b_ref, o_ref):
    o_ref[...] = _dot_hi(_silu(c_ref[...]), w_ref[...]) + b_ref[...]


def _ada_mods(c, ada_w, ada_b):
    B, D = c.shape
    depth, _, N = ada_w.shape
    rows = 8
    cp = jnp.zeros((rows, D), F32).at[:B].set(c)
    tn = N // 8
    out = pl.pallas_call(
        _ada_kernel,
        grid=(depth, N // tn),
        in_specs=[
            pl.BlockSpec((rows, D), lambda l, n: (0, 0)),
            pl.BlockSpec((None, D, tn), lambda l, n: (l, 0, n)),
            pl.BlockSpec((None, 1, tn), lambda l, n: (l, 0, n)),
        ],
        out_specs=pl.BlockSpec((None, rows, tn), lambda l, n: (l, 0, n)),
        out_shape=jax.ShapeDtypeStruct((depth, rows, N), F32),
        compiler_params=_cparams(("parallel", "parallel")),
        name="ada_mods",
    )(cp, ada_w, ada_b.reshape(depth, 1, N))
    return out[:, :B].reshape(depth, B, 9, D)


def _ffn_kernel(x_ref, m_ref, g_ref, wg_ref, wu_ref, wo_ref, o_ref, *, sub, tf):
    x = x_ref[...]
    shift = m_ref[3 * sub:3 * sub + 1, :]
    scale = m_ref[3 * sub + 1:3 * sub + 2, :]
    gate = m_ref[3 * sub + 2:3 * sub + 3, :]
    h = _modnorm(x, g_ref[...], shift, scale).astype(BF16)
    acc = jnp.zeros(x.shape, F32)
    for j in range(wg_ref.shape[1] // tf):
        gt = _dot(h, wg_ref[:, j * tf:(j + 1) * tf])
        up = _dot(h, wu_ref[:, j * tf:(j + 1) * tf])
        a = (_silu(gt) * up).astype(BF16)
        acc = acc + _dot(a, wo_ref[j * tf:(j + 1) * tf, :])
    o_ref[...] = x + (MACARON_W * gate) * acc


def _ffn(x, mods, g, w_in, w_out, sub, tm=512):
    B, S, D = x.shape
    dff = w_out.shape[0]
    tm = min(tm, S)
    tf = dff // 2
    return pl.pallas_call(
        functools.partial(_ffn_kernel, sub=sub, tf=tf),
        grid=(B, S // tm),
        in_specs=[
            pl.BlockSpec((None, tm, D), lambda b, i: (b, i, 0)),
            pl.BlockSpec((None, 9, D), lambda b, i: (b, 0, 0)),
            pl.BlockSpec((1, D), lambda b, i: (0, 0)),
            _resident((D, dff), lambda b, i: (0, 0)),
            _resident((D, dff), lambda b, i: (0, 1)),
            _resident((dff, D), lambda b, i: (0, 0)),
        ],
        out_specs=pl.BlockSpec((None, tm, D), lambda b, i: (b, i, 0)),
        out_shape=jax.ShapeDtypeStruct((B, S, D), F32),
        compiler_params=_cparams(("parallel", "parallel")),
        name="ffn",
    )(x, mods, g.reshape(1, D), w_in, w_in, w_out)


def _proj_kernel(x_ref, m_ref, g_ref, w_ref, *o_and_scr, sub):
    o_refs, h_scr = o_and_scr[:-1], o_and_scr[-1]

    @pl.when(pl.program_id(2) == 0)
    def _():
        h_scr[...] = _modnorm(x_ref[...], g_ref[...], m_ref[3 * sub:3 * sub + 1, :],
                              m_ref[3 * sub + 1:3 * sub + 2, :]).astype(BF16)

    y = _dot(h_scr[...], w_ref[...])
    for o_ref in o_refs:
        o_ref[...] = y.astype(o_ref.dtype)


def _proj(x, mods, g, w, out_dtypes, sub=1, tm=1024, tn=1024):
    B, S, D = x.shape
    N = w.shape[1]
    tm = min(tm, S)
    tn = tn if N % tn == 0 else N
    outs = pl.pallas_call(
        functools.partial(_proj_kernel, sub=sub),
        grid=(B, S // tm, N // tn),
        in_specs=[
            pl.BlockSpec((None, tm, D), lambda b, i, n: (b, i, 0)),
            pl.BlockSpec((None, 9, D), lambda b, i, n: (b, 0, 0)),
            pl.BlockSpec((1, D), lambda b, i, n: (0, 0)),
            pl.BlockSpec((D, tn), lambda b, i, n: (0, n)),
        ],
        out_specs=[pl.BlockSpec((None, tm, tn), lambda b, i, n: (b, i, n)) for _ in out_dtypes],
        out_shape=[jax.ShapeDtypeStruct((B, S, N), dt) for dt in out_dtypes],
        scratch_shapes=[pltpu.VMEM((tm, D), BF16)],
        compiler_params=_cparams(("parallel", "parallel", "arbitrary")),
        name="modnorm_proj",
    )(x, mods, g.reshape(1, D), w)
    return outs


def _proj_t_kernel(x_ref, m_ref, g_ref, wt_ref, o_ref, *, sub):
    h = _modnorm(x_ref[...], g_ref[...], m_ref[3 * sub:3 * sub + 1, :],
                 m_ref[3 * sub + 1:3 * sub + 2, :]).astype(BF16)
    o_ref[...] = _dot_nt(wt_ref[...], h).astype(o_ref.dtype)


def _proj_t(x, mods, g, wt, out_dtype, sub=1, tm=1024):
    B, S, D = x.shape
    N = wt.shape[0]
    tm = min(tm, S)
    return pl.pallas_call(
        functools.partial(_proj_t_kernel, sub=sub),
        grid=(B, S // tm),
        in_specs=[
            pl.BlockSpec((None, tm, D), lambda b, i: (b, i, 0)),
            pl.BlockSpec((None, 9, D), lambda b, i: (b, 0, 0)),
            pl.BlockSpec((1, D), lambda b, i: (0, 0)),
            _resident((N, D), lambda b, i: (0, 0)),
        ],
        out_specs=pl.BlockSpec((None, N, tm), lambda b, i: (b, 0, i)),
        out_shape=jax.ShapeDtypeStruct((B, N, S), out_dtype),
        compiler_params=_cparams(("parallel", "parallel")),
        name="modnorm_proj_t",
    )(x, mods, g.reshape(1, D), wt)


def _norm_kernel(x_ref, m_ref, g_ref, o_ref, *, sub):
    o_ref[...] = _modnorm(x_ref[...], g_ref[...], m_ref[3 * sub:3 * sub + 1, :],
                          m_ref[3 * sub + 1:3 * sub + 2, :])


def _modnorm_call(x, mods, g, sub=1, tm=1024):
    B, S, D = x.shape
    tm = min(tm, S)
    return pl.pallas_call(
        functools.partial(_norm_kernel, sub=sub),
        grid=(B, S // tm),
        in_specs=[
            pl.BlockSpec((None, tm, D), lambda b, i: (b, i, 0)),
            pl.BlockSpec((None, 9, D), lambda b, i: (b, 0, 0)),
            pl.BlockSpec((1, D), lambda b, i: (0, 0)),
        ],
        out_specs=pl.BlockSpec((None, tm, D), lambda b, i: (b, i, 0)),
        out_shape=jax.ShapeDtypeStruct((B, S, D), F32),
        compiler_params=_cparams(("parallel", "parallel")),
        name="modnorm",
    )(x, mods, g.reshape(1, D))


def _out_kernel(x_ref, y_ref, m_ref, w_ref, o_ref):
    o_ref[...] = x_ref[...] + m_ref[5:6, :] * _dot(y_ref[...], w_ref[...])


def _out_proj(x, y, mods, w, tm=512):
    B, S, D = x.shape
    K = w.shape[0]
    tm = min(tm, S)
    return pl.pallas_call(
        _out_kernel,
        grid=(B, S // tm),
        in_specs=[
            pl.BlockSpec((None, tm, D), lambda b, i: (b, i, 0)),
            pl.BlockSpec((None, tm, K), lambda b, i: (b, i, 0)),
            pl.BlockSpec((None, 9, D), lambda b, i: (b, 0, 0)),
            _resident((K, D), lambda b, i: (0, 0)),
        ],
        out_specs=pl.BlockSpec((None, tm, D), lambda b, i: (b, i, 0)),
        out_shape=jax.ShapeDtypeStruct((B, S, D), F32),
        compiler_params=_cparams(("parallel", "parallel")),
        name="out_proj",
    )(x, y, mods, w)


def _final_kernel(x_ref, g_ref, o_ref):
    o_ref[...] = _rms(x_ref[...], g_ref[...])


def _final_norm(x, g, tm=1024):
    B, S, D = x.shape
    tm = min(tm, S)
    return pl.pallas_call(
        _final_kernel,
        grid=(B, S // tm),
        in_specs=[pl.BlockSpec((None, tm, D), lambda b, i: (b, i, 0)),
                  pl.BlockSpec((1, D), lambda b, i: (0, 0))],
        out_specs=pl.BlockSpec((None, tm, D), lambda b, i: (b, i, 0)),
        out_shape=jax.ShapeDtypeStruct((B, S, D), F32),
        compiler_params=_cparams(("parallel", "parallel")),
        name="final_norm",
    )(x, g.reshape(1, D))


def _rope_tables(seq, dim):
    inv = ROPE_BASE ** (-jnp.arange(0, dim, 2, dtype=F32) / dim)
    ang = jnp.arange(seq, dtype=F32)[:, None] * inv[None, :]
    return jnp.cos(ang), jnp.sin(ang)


def _mla_prep_kernel(p_ref, qn_ref, kvn_ref, wq_ref, wqs_ref, wk_ref, wvt_ref, c_ref, s_ref,
                     q_ref, k_ref, vt_ref):
    p = p_ref[...]
    cq = _rms(p[:, :MLA_Q_RANK], qn_ref[...]).astype(BF16)
    ckv = _rms(p[:, MLA_Q_RANK:MLA_Q_RANK + MLA_KV_RANK], kvn_ref[...]).astype(BF16)
    o = MLA_Q_RANK + MLA_KV_RANK
    cos = c_ref[...]
    sin = s_ref[...]
    k_rope = p[:, o:o + LANES] * cos + p[:, o + LANES:o + 2 * LANES] * sin
    q1 = _dot(cq, wq_ref[...])
    q2 = _dot(cq, wqs_ref[...])
    kn = _dot(ckv, wk_ref[...])
    scale = (MLA_NOPE + MLA_ROPE) ** -0.5
    for h in range(MLA_HEADS):
        sl = slice(h * LANES, (h + 1) * LANES)
        q_ref[:, sl] = ((q1[:, sl] * cos + q2[:, sl] * sin) * scale).astype(BF16)
        k_ref[:, sl] = (kn[:, sl] + k_rope).astype(BF16)
    vt_ref[...] = _dot_nt(wvt_ref[...], ckv).astype(BF16)


def _flash_update(s_t, vt, m_scr, l_scr, acc_scr, h):
    m_old = m_scr[h]
    m_new = jnp.maximum(m_old, jnp.max(s_t, axis=0, keepdims=True))
    alpha = jnp.exp(m_old - m_new)
    p_t = jnp.exp(s_t - m_new)
    l_scr[h] = alpha * l_scr[h] + jnp.sum(p_t, axis=0, keepdims=True)
    acc_scr[h] = alpha * acc_scr[h] + _dot(vt, p_t.astype(BF16))
    m_scr[h] = m_new


def _flash_init(m_scr, l_scr, acc_scr):
    m_scr[...] = jnp.full(m_scr.shape, -jnp.inf, F32)
    l_scr[...] = jnp.zeros(l_scr.shape, F32)
    acc_scr[...] = jnp.zeros(acc_scr.shape, F32)


def _flash_finish(o_ref, l_scr, acc_scr):
    o_t = jnp.concatenate([acc_scr[0] / l_scr[0], acc_scr[1] / l_scr[1]], axis=0)
    o_ref[...] = o_t.T.astype(o_ref.dtype)


def _mla_attn_kernel(q_ref, k_ref, vt_ref, o_ref, s_scr, m_scr, l_scr, acc_scr, *, t):
    i = pl.program_id(2)
    _flash_init(m_scr, l_scr, acc_scr)
    q = q_ref[...]

    def scores(j, slot):
        kb = k_ref[pl.ds(pl.multiple_of(j * t, t), t), :]
        for h in range(2):
            s_scr[slot, h] = _dot_nt(kb[:, h * LANES:(h + 1) * LANES], q[:, h * LANES:(h + 1) * LANES])

    def consume(j, slot, diagonal):
        vtb = vt_ref[:, pl.ds(pl.multiple_of(j * t, t), t)]
        for h in range(2):
            s_t = s_scr[slot, h]
            if diagonal:
                key = lax.broadcasted_iota(jnp.int32, s_t.shape, 0)
                qry = lax.broadcasted_iota(jnp.int32, s_t.shape, 1)
                s_t = jnp.where(key <= qry, s_t, -jnp.inf)
            _flash_update(s_t, vtb[h * MLA_V:(h + 1) * MLA_V, :], m_scr, l_scr, acc_scr, h)

    scores(0, 0)

    def body(p, carry):
        j = 2 * p
        scores(j + 1, 1)
        consume(j, 0, False)
        scores(j + 2, 0)
        consume(j + 1, 1, False)
        return carry

    lax.fori_loop(0, i // 2, body, 0)

    @pl.when(i % 2 == 0)
    def _():
        consume(i, 0, True)

    @pl.when(i % 2 == 1)
    def _():
        scores(i, 1)
        consume(i - 1, 0, False)
        consume(i, 1, True)

    _flash_finish(o_ref, l_scr, acc_scr)


def _mla_weights(w_in, w_uq, w_ukv):
    H, dn, dr, dv = MLA_HEADS, MLA_NOPE, MLA_ROPE, MLA_V
    D = w_in.shape[0]
    o = MLA_Q_RANK + MLA_KV_RANK
    kr = w_in[:, o:]
    z64 = jnp.zeros((D, dn), F32)
    z32 = jnp.zeros((D, LANES - dn - dr), F32)
    half = dr // 2
    w_in_p = jnp.concatenate(
        [w_in[:, :o], z64, kr, z32, z64, -kr[:, half:], kr[:, :half], z32], axis=1)
    wq = w_uq.reshape(MLA_Q_RANK, H, dn + dr)
    zq = jnp.zeros((MLA_Q_RANK, H, LANES - dn - dr), F32)
    wq_p = jnp.concatenate([wq, zq], axis=2).reshape(MLA_Q_RANK, H * LANES)
    wqs_p = jnp.concatenate(
        [jnp.zeros((MLA_Q_RANK, H, dn), F32), -wq[:, :, dn + half:], wq[:, :, dn:dn + half], zq],
        axis=2).reshape(MLA_Q_RANK, H * LANES)
    wkv = w_ukv.reshape(MLA_KV_RANK, H, dn + dv)
    wk_p = jnp.concatenate(
        [wkv[:, :, :dn], jnp.zeros((MLA_KV_RANK, H, LANES - dn), F32)], axis=2).reshape(MLA_KV_RANK, H * LANES)
    wvt_p = wkv[:, :, dn:].reshape(MLA_KV_RANK, H * dv).T
    return (w_in_p.astype(BF16), wq_p.astype(BF16), wqs_p.astype(BF16),
            wk_p.astype(BF16), wvt_p.astype(BF16))


def _mla_mixer(x, mods, g, w_in, q_norm, w_uq, kv_norm, w_ukv, w_out, tm=512, t=256):
    B, S, D = x.shape
    H = MLA_HEADS
    tm = min(tm, S)
    t = min(t, S)
    w_in_p, wq_p, wqs_p, wk_p, wvt_p = _mla_weights(w_in, w_uq, w_ukv)
    (p,) = _proj(x, mods, g, w_in_p, [F32])
    cos, sin = _rope_tables(S, MLA_ROPE)
    one = jnp.ones((S, MLA_NOPE), F32)
    pad = LANES - MLA_NOPE - MLA_ROPE
    tab_c = jnp.concatenate([one, cos, cos, jnp.ones((S, pad), F32)], axis=1)
    tab_s = jnp.concatenate([0 * one, sin, sin, jnp.zeros((S, pad), F32)], axis=1)
    NP = p.shape[-1]
    q, k, vt = pl.pallas_call(
        _mla_prep_kernel,
        grid=(B, S // tm),
        in_specs=[
            pl.BlockSpec((None, tm, NP), lambda b, i: (b, i, 0)),
            pl.BlockSpec((1, MLA_Q_RANK), lambda b, i: (0, 0)),
            pl.BlockSpec((1, MLA_KV_RANK), lambda b, i: (0, 0)),
            _resident(wq_p.shape, lambda b, i: (0, 0)),
            _resident(wqs_p.shape, lambda b, i: (0, 0)),
            _resident(wk_p.shape, lambda b, i: (0, 0)),
            _resident(wvt_p.shape, lambda b, i: (0, 0)),
            pl.BlockSpec((tm, LANES), lambda b, i: (i, 0)),
            pl.BlockSpec((tm, LANES), lambda b, i: (i, 0)),
        ],
        out_specs=[
            pl.BlockSpec((None, tm, H * LANES), lambda b, i: (b, i, 0)),
            pl.BlockSpec((None, tm, H * LANES), lambda b, i: (b, i, 0)),
            pl.BlockSpec((None, H * MLA_V, tm), lambda b, i: (b, 0, i)),
        ],
        out_shape=[
            jax.ShapeDtypeStruct((B, S, H * LANES), BF16),
            jax.ShapeDtypeStruct((B, S, H * LANES), BF16),
            jax.ShapeDtypeStruct((B, H * MLA_V, S), BF16),
        ],
        compiler_params=_cparams(("parallel", "parallel")),
        name="mla_prep",
    )(p, q_norm.reshape(1, -1), kv_norm.reshape(1, -1), wq_p, wqs_p, wk_p, wvt_p, tab_c, tab_s)
    o = pl.pallas_call(
        functools.partial(_mla_attn_kernel, t=t),
        grid=(B, H // 2, S // t),
        in_specs=[
            pl.BlockSpec((None, t, 2 * LANES), lambda b, hp, i: (b, i, hp)),
            pl.BlockSpec((None, S, 2 * LANES), lambda b, hp, i: (b, 0, hp)),
            pl.BlockSpec((None, LANES, S), lambda b, hp, i: (b, hp, 0)),
        ],
        out_specs=pl.BlockSpec((None, t, LANES), lambda b, hp, i: (b, i, hp)),
        out_shape=jax.ShapeDtypeStruct((B, S, H * MLA_V), BF16),
        scratch_shapes=[pltpu.VMEM((2, 2, t, t), F32),
                        pltpu.VMEM((2, 1, t), F32), pltpu.VMEM((2, 1, t), F32),
                        pltpu.VMEM((2, MLA_V, t), F32)],
        compiler_params=_cparams(("parallel", "parallel", "arbitrary")),
        name="mla_attn",
    )(q, k, vt)
    return _out_proj(x, o, mods, w_out.astype(BF16))


def _moba_kmean_kernel(k_ref, o_ref):
    o_ref[...] = jnp.mean(k_ref[...], axis=0, keepdims=True)


def _moba_select_kernel(q_ref, km_ref, o_ref, *, nb):
    H, dh, tq = MOBA_HEADS, MOBA_HEAD_DIM, MOBA_BLOCK
    own = pl.program_id(1)
    km = km_ref[...]
    kbd = jnp.concatenate([km] * H, axis=0)
    assert nb & (nb - 1) == 0 and dh & (dh - 1) == 0
    row_head = lax.shift_right_logical(lax.broadcasted_iota(jnp.int32, kbd.shape, 0), nb.bit_length() - 1)
    lane_head = lax.shift_right_logical(lax.broadcasted_iota(jnp.int32, kbd.shape, 1), dh.bit_length() - 1)
    kbd = jnp.where(row_head == lane_head, kbd, 0.0)
    gate = _dot_nt(kbd, q_ref[...], precision=HI)
    g = gate.reshape(H, nb, tq)
    n_iota = lax.broadcasted_iota(jnp.int32, g.shape, 1)
    g = jnp.where(n_iota < own, g, -jnp.inf)
    sel = jnp.zeros(g.shape, jnp.bool_)
    for _ in range(min(MOBA_TOPK, nb)):
        m = jnp.max(g, axis=1, keepdims=True)
        first = jnp.min(jnp.where(g == m, n_iota, nb), axis=1, keepdims=True)
        pick = n_iota == first
        sel = sel | (pick & (m > -jnp.inf))
        g = jnp.where(pick, -jnp.inf, g)
    bias = jnp.where(sel | (n_iota == own), 0.0, NEG_BIG)
    if nb < LANES:
        bias = jnp.concatenate([bias, jnp.zeros((H, LANES - nb, tq), F32)], axis=1)
    o_ref[...] = bias.reshape(H * LANES, tq).T.astype(o_ref.dtype)


def _moba_bias_kernel(tbl_ref, o_ref):
    h = pl.program_id(0)
    blk = MOBA_BLOCK
    key = lax.broadcasted_iota(jnp.int32, (blk, blk), 0)
    qry = lax.broadcasted_iota(jnp.int32, (blk, blk), 1)
    max_exact = REL_BUCKETS // 2
    for t in range(2):
        n = jnp.maximum(qry - key + blk * t, 0)
        nf = jnp.maximum(n, max_exact).astype(F32)
        large = max_exact + (jnp.log(nf / max_exact) / math.log(REL_MAX_DIST / max_exact)
                             * (REL_BUCKETS - max_exact)).astype(jnp.int32)
        large = jnp.minimum(large, REL_BUCKETS - 1)
        bucket = jnp.where(n < max_exact, n, large)
        acc = jnp.zeros((blk, blk), F32)
        for b in range(REL_BUCKETS):
            acc = jnp.where(bucket == b, tbl_ref[b, h], acc)
        o_ref[t] = acc


def _moba_attn_kernel(tbl_ref, q_ref, k_ref, vt_ref, sel_ref, bias_ref, o_ref,
                      s_scr, m_scr, l_scr, acc_scr):
    hp = pl.program_id(1)
    i = pl.program_id(2)
    t = MOBA_BLOCK
    dh = MOBA_HEAD_DIM
    _flash_init(m_scr, l_scr, acc_scr)
    q = q_ref[...]
    lane = lax.broadcasted_iota(jnp.int32, q.shape, 1)
    scale = dh ** -0.5
    qh = [jnp.where((lane < dh) == (h == 0), q, 0) * scale for h in range(2)]
    qa = [jnp.concatenate([qh[h], sel_ref[:, h * LANES:(h + 1) * LANES]], axis=1) for h in range(2)]

    def scores(j, slot):
        kb = k_ref[pl.ds(pl.multiple_of(j * t, t), t), :]
        onehot = jnp.where(lane == j, 1.0, 0.0).astype(BF16)
        ka = jnp.concatenate([kb, onehot], axis=1)
        for h in range(2):
            s_scr[slot, h] = _dot_nt(ka, qa[h])

    FAR, PREV, OWN = 0, 1, 2

    def consume(j, slot, kind):
        vtb = vt_ref[:, pl.ds(pl.multiple_of(j * t, t), t)]
        for h in range(2):
            if kind == FAR:
                s_t = s_scr[slot, h] + tbl_ref[REL_BUCKETS - 1, 2 * hp + h]
            elif kind == PREV:
                s_t = s_scr[slot, h] + bias_ref[h, 1]
            else:
                key = lax.broadcasted_iota(jnp.int32, (t, t), 0)
                qry = lax.broadcasted_iota(jnp.int32, (t, t), 1)
                s_t = jnp.where(key <= qry, s_scr[slot, h] + bias_ref[h, 0], -jnp.inf)
            _flash_update(s_t, vtb[h * dh:(h + 1) * dh, :], m_scr, l_scr, acc_scr, h)

    scores(0, 0)
    n_far = jnp.maximum(i - 1, 0)

    def body(p, carry):
        j = 2 * p
        scores(j + 1, 1)
        consume(j, 0, FAR)
        scores(j + 2, 0)
        consume(j + 1, 1, FAR)
        return carry

    lax.fori_loop(0, n_far // 2, body, 0)

    @pl.when(i == 0)
    def _():
        consume(i, 0, OWN)

    @pl.when(i % 2 == 1)
    def _():
        scores(i, 1)
        consume(i - 1, 0, PREV)
        consume(i, 1, OWN)

    @pl.when((i % 2 == 0) & (i > 0))
    def _():
        scores(i - 1, 1)
        consume(i - 2, 0, FAR)
        scores(i, 0)
        consume(i - 1, 1, PREV)
        consume(i, 0, OWN)

    _flash_finish(o_ref, l_scr, acc_scr)


def _moba_mixer(x, mods, g, w_in, rel_table, w_out):
    B, S, D = x.shape
    H, t = MOBA_HEADS, MOBA_BLOCK
    nb = S // t
    w_in16 = w_in.astype(BF16)
    qkv, qkv16 = _proj(x, mods, g, w_in16[:, :2 * D], [F32, BF16])
    vt = _proj_t(x, mods, g, w_in16[:, 2 * D:].T, BF16)
    kmean = pl.pallas_call(
        _moba_kmean_kernel,
        grid=(B, nb),
        in_specs=[pl.BlockSpec((None, t, D), lambda b, n: (b, n, 1))],
        out_specs=pl.BlockSpec((None, None, 1, D), lambda b, n: (b, n, 0, 0)),
        out_shape=jax.ShapeDtypeStruct((B, nb, 1, D), F32),
        compiler_params=_cparams(("parallel", "parallel")),
        name="moba_kmean",
    )(qkv).reshape(B, nb, D)
    sel = pl.pallas_call(
        functools.partial(_moba_select_kernel, nb=nb),
        grid=(B, nb),
        in_specs=[pl.BlockSpec((None, t, D), lambda b, i: (b, i, 0)),
                  pl.BlockSpec((None, nb, D), lambda b, i: (b, 0, 0))],
        out_specs=pl.BlockSpec((None, t, H * LANES), lambda b, i: (b, i, 0)),
        out_shape=jax.ShapeDtypeStruct((B, S, H * LANES), BF16),
        compiler_params=_cparams(("parallel", "parallel")),
        name="moba_select",
    )(qkv, kmean)
    bias = pl.pallas_call(
        _moba_bias_kernel,
        grid=(H,),
        in_specs=[pl.BlockSpec(memory_space=pltpu.SMEM)],
        out_specs=pl.BlockSpec((None, 2, t, t), lambda h: (h, 0, 0, 0)),
        out_shape=jax.ShapeDtypeStruct((H, 2, t, t), F32),
        compiler_params=_cparams(("parallel",)),
        name="moba_bias",
    )(rel_table)
    nhp = H // 2
    o = pl.pallas_call(
        _moba_attn_kernel,
        grid=(B, nhp, nb),
        in_specs=[
            pl.BlockSpec(memory_space=pltpu.SMEM),
            pl.BlockSpec((None, t, LANES), lambda b, hp, i: (b, i, hp)),
            pl.BlockSpec((None, S, LANES), lambda b, hp, i: (b, 0, nhp + hp)),
            pl.BlockSpec((None, LANES, S), lambda b, hp, i: (b, hp, 0)),
            pl.BlockSpec((None, t, 2 * LANES), lambda b, hp, i: (b, i, hp)),
            pl.BlockSpec((2, 2, t, t), lambda b, hp, i: (hp, 0, 0, 0)),
        ],
        out_specs=pl.BlockSpec((None, t, LANES), lambda b, hp, i: (b, i, hp)),
        out_shape=jax.ShapeDtypeStruct((B, S, D), BF16),
        scratch_shapes=[pltpu.VMEM((2, 2, t, t), F32),
                        pltpu.VMEM((2, 1, t), F32), pltpu.VMEM((2, 1, t), F32),
                        pltpu.VMEM((2, MOBA_HEAD_DIM, t), F32)],
        compiler_params=_cparams(("parallel", "parallel", "arbitrary")),
        name="moba_attn",
    )(rel_table, qkv16, qkv16, vt, sel, bias)
    return _out_proj(x, o, mods, w_out.astype(BF16))


def _ret_kernel(gam_ref, q_ref, k_ref, v_ref, g_ref, cos_ref, sin_ref, dm_ref, zeta_ref, xi_ref,
                gnw_ref, gnb_ref, o_ref, r_scr):
    h = pl.program_id(1)

    @pl.when(pl.program_id(2) == 0)
    def _():
        r_scr[...] = jnp.zeros(r_scr.shape, F32)

    cos = cos_ref[...]
    sin = sin_ref[...]
    half = RET_DK // 2

    def rope(t):
        t1, t2 = t[:, :half], t[:, half:]
        return jnp.concatenate([t1 * cos - t2 * sin, t1 * sin + t2 * cos], axis=1)

    q = rope(q_ref[...])
    k = rope(k_ref[...] * RET_DK ** -0.5)
    v = v_ref[...]
    R = r_scr[...]
    inner = _dot_nt(q, k, precision=HI) * dm_ref[...]
    o = _dot_hi(inner, v) + _dot_hi(q, R) * xi_ref[...]
    r_scr[...] = gam_ref[h] * R + _dot_hi((k * zeta_ref[...]).T, v)
    mu = jnp.mean(o, axis=-1, keepdims=True)
    d = o - mu
    var = jnp.mean(d * d, axis=-1, keepdims=True)
    yn = (d * lax.rsqrt(var + RET_GN_EPS)) * gnw_ref[...] + gnb_ref[...]
    o_ref[...] = (_silu(g_ref[...]) * yn).astype(o_ref.dtype)


def _retnet_mixer(x, mods, g, w_in, gn_w, gn_b, w_out):
    B, S, D = x.shape
    H, DK, DV, C = RET_HEADS, RET_DK, RET_DV, RET_CHUNK
    C = min(C, S)
    (p,) = _proj(x, mods, g, w_in.astype(BF16), [F32])
    cos, sin = _rope_tables(S, DK)
    log_gamma = jnp.log1p(-jnp.exp2(-5.0 - jnp.arange(H, dtype=F32)))
    idx = jnp.arange(C, dtype=F32)
    diff = idx[:, None] - idx[None, :]
    dmask = jnp.where(diff >= 0, jnp.exp(jnp.maximum(diff, 0.0)[None] * log_gamma[:, None, None]), 0.0)
    zeta = jnp.exp((C - 1 - idx)[None, :] * log_gamma[:, None])[:, :, None]
    xi = jnp.exp((idx + 1)[None, :] * log_gamma[:, None])[:, :, None]
    gamma_c = jnp.exp(C * log_gamma)
    kq, kk, kv, kg = 0, D // DK, 2 * D // DV, 4 * D // DV
    y = pl.pallas_call(
        _ret_kernel,
        grid=(B, H, S // C),
        in_specs=[
            pl.BlockSpec(memory_space=pltpu.SMEM),
            pl.BlockSpec((None, C, DK), lambda b, h, c: (b, c, kq + h)),
            pl.BlockSpec((None, C, DK), lambda b, h, c: (b, c, kk + h)),
            pl.BlockSpec((None, C, DV), lambda b, h, c: (b, c, kv + h)),
            pl.BlockSpec((None, C, DV), lambda b, h, c: (b, c, kg + h)),
            pl.BlockSpec((C, DK // 2), lambda b, h, c: (c, 0)),
            pl.BlockSpec((C, DK // 2), lambda b, h, c: (c, 0)),
            pl.BlockSpec((None, C, C), lambda b, h, c: (h, 0, 0)),
            pl.BlockSpec((None, C, 1), lambda b, h, c: (h, 0, 0)),
            pl.BlockSpec((None, C, 1), lambda b, h, c: (h, 0, 0)),
            pl.BlockSpec((1, DV), lambda b, h, c: (0, h)),
            pl.BlockSpec((1, DV), lambda b, h, c: (0, h)),
        ],
        out_specs=pl.BlockSpec((None, C, DV), lambda b, h, c: (b, c, h)),
        out_shape=jax.ShapeDtypeStruct((B, S, H * DV), BF16),
        scratch_shapes=[pltpu.VMEM((DK, DV), F32)],
        compiler_params=_cparams(("parallel", "parallel", "arbitrary")),
        name="retention",
    )(gamma_c, p, p, p, p, cos, sin, dmask, zeta, xi, gn_w.reshape(1, -1), gn_b.reshape(1, -1))
    return _out_proj(x, y, mods, w_out.astype(BF16))


def _head_sum(t, gs_ref, gst_ref):
    return _dot_hi(_dot_hi(t, gs_ref[...]), gst_ref[...])


def _rwkv_prep_kernel(h_ref, hp_ref, mu_ref, wrkv_ref, w0_ref, wd1_ref, wd2_ref, a0_ref, wa1_ref,
                      wa2_ref, wg1_ref, wg2_ref, kk_ref, ka_ref, rk_ref, gs_ref, gst_ref,
                      r_o, lw_o, k_o, v_o, kk_o, a_o, g_o, bonus_o):
    i = pl.program_id(1)
    h = h_ref[...]
    tm = h.shape[0]
    prev_tail = jnp.where(i > 0, hp_ref[7:8, :], 0.0)
    rolled = pltpu.roll(h, 1, axis=0)
    row = lax.broadcasted_iota(jnp.int32, h.shape, 0)
    xx = jnp.where(row == 0, prev_tail, rolled) - h
    xs = [h + xx * mu_ref[n:n + 1, :] for n in range(6)]
    r = _dot(xs[0].astype(BF16), wrkv_ref[0])
    k = _dot(xs[1].astype(BF16), wrkv_ref[1])
    v = _dot(xs[2].astype(BF16), wrkv_ref[2])
    z = -(w0_ref[...] + _dot_hi(jnp.tanh(_dot_hi(xs[3], wd1_ref[...])), wd2_ref[...]))
    softplus = jnp.maximum(z, 0.0) + jnp.log(1.0 + jnp.exp(-jnp.abs(z)))
    w = -softplus - 0.5
    lw_o[...] = -jnp.exp(w)
    a = jax.nn.sigmoid(a0_ref[...] + _dot_hi(_dot_hi(xs[4], wa1_ref[...]), wa2_ref[...]))
    g_o[...] = _dot_hi(jax.nn.sigmoid(_dot_hi(xs[5], wg1_ref[...])), wg2_ref[...])
    kk = k * kk_ref[...]
    norm = jnp.sqrt(_head_sum(kk * kk, gs_ref, gst_ref))
    kk_o[...] = kk / jnp.maximum(norm, 1e-12)
    k_mod = k * (1 + (a - 1) * ka_ref[...])
    r_o[...] = r
    k_o[...] = k_mod
    v_o[...] = v
    a_o[...] = a
    bonus_o[...] = _head_sum(r * k_mod * rk_ref[...], gs_ref, gst_ref) * v


def _rwkv_scan_kernel(r_ref, lw_ref, k_ref, v_ref, kk_ref, a_ref, y_ref, st_scr, *, pairs):
    C = RWKV_CHUNK
    P = 2 * C

    @pl.when(pl.program_id(2) == 0)
    def _():
        st_scr[...] = jnp.zeros(st_scr.shape, F32)

    tri = (lax.broadcasted_iota(jnp.int32, (C, C), 0) >= lax.broadcasted_iota(jnp.int32, (C, C), 1))
    tri = jnp.where(tri, 1.0, 0.0).astype(BF16)
    lane = lax.broadcasted_iota(jnp.int32, (C, LANES), 1)
    first = lane < RWKV_HEAD
    rt = lax.broadcasted_iota(jnp.int32, (P, P), 0) & (C - 1)
    ct = lax.broadcasted_iota(jnp.int32, (P, P), 1) & (C - 1)
    strict = rt > ct
    incl = rt >= ct
    eye = jnp.where(lax.broadcasted_iota(jnp.int32, (P, P), 0) == lax.broadcasted_iota(jnp.int32, (P, P), 1),
                    1.0, 0.0)

    def stack(t):
        return jnp.concatenate([jnp.where(first, t, 0.0), jnp.where(first, 0.0, t)], axis=0)

    for p in range(pairs):
        sl = slice(p * LANES, (p + 1) * LANES)
        lw = lw_ref[:, sl]
        kk = kk_ref[:, sl]
        lw_hi, lw_mid = _split(lw)
        lw_lo = (lw - lw_hi.astype(F32) - lw_mid.astype(F32)).astype(BF16)
        cum = _dot(tri, lw_hi) + (_dot(tri, lw_mid) + _dot(tri, lw_lo))
        p_inc = jnp.exp(cum)
        p_exc = jnp.exp(cum - lw)
        p_inv = jnp.exp(-cum)
        a_s = stack(-kk * p_exc)
        b_s = stack(kk * a_ref[:, sl] * p_inv)
        k_s = stack(k_ref[:, sl] * p_inv)
        r_s = stack(r_ref[:, sl] * p_inc)
        v_s = stack(v_ref[:, sl])
        ar = _split(jnp.concatenate([a_s, r_s], axis=0))
        bk = _split(jnp.concatenate([b_s, k_s], axis=0))
        v_f = v_s
        v_s = _split(v_f)
        gram = _dot3_nt(ar, bk)
        l_ab = jnp.where(strict, gram[:P, :P], 0.0)
        l_ak = jnp.where(strict, gram[:P, P:], 0.0)
        a_rb = jnp.where(incl, gram[P:, :P], 0.0)
        a_rk = jnp.where(incl, gram[P:, P:], 0.0)
        t_inv = eye + l_ab
        lp = _split(l_ab)
        for _ in range(int(math.log2(C)) - 1):
            lp = _split(_dot3(lp, lp))
            t_inv = t_inv + _dot3(_split(t_inv), lp)
        st = st_scr[p]
        arh = _dot3_nt(ar, _split(st))
        u = _dot3(_split(t_inv), _split(arh[:P] + _dot3(_split(l_ak), v_s)))
        u_s = _split(u)
        o_s = arh[P:] + _dot3(_split(a_rb), u_s) + _dot3(_split(a_rk), v_s)
        y_ref[:, sl] = o_s[:C] + o_s[C:]
        uv_t = _split(jnp.concatenate([u, v_f], axis=0).T)
        st_scr[p] = (st + _dot3(uv_t, bk)) * p_inc[C - 1:C, :]


def _rwkv_post_kernel(x_ref, y_ref, bonus_ref, g_ref, m_ref, gnw_ref, gnb_ref, gs_ref, gst_ref,
                      w_ref, o_ref):
    y = y_ref[...]
    inv_n = 1.0 / RWKV_HEAD
    mu = _head_sum(y, gs_ref, gst_ref) * inv_n
    d = y - mu
    var = _head_sum(d * d, gs_ref, gst_ref) * inv_n
    yn = (d * lax.rsqrt(var + RWKV_GN_EPS)) * gnw_ref[...] + gnb_ref[...]
    z = ((yn + bonus_ref[...]) * g_ref[...]).astype(BF16)
    o_ref[...] = x_ref[...] + m_ref[5:6, :] * _dot(z, w_ref[...])


def _rwkv7_mixer(x, mods, g, mu, w_rkv, w0, wd1, wd2, a0, wa1, wa2, wg1, wg2, k_k, k_a, r_k,
                 gn_w, gn_b, w_out, tm=256, pairs=4):
    B, S, D = x.shape
    H, N, C = RWKV_HEADS, RWKV_HEAD, RWKV_CHUNK
    tm = min(tm, S)
    h = _modnorm_call(x, mods, g)
    gs = (jnp.arange(D)[:, None] // N == jnp.arange(LANES)[None, :]).astype(F32)
    H = LANES

    def pad_rank(w1, w2):
        r = w1.shape[1]
        rp = -(-r // LANES) * LANES
        return jnp.pad(w1, ((0, 0), (0, rp - r))), jnp.pad(w2, ((0, rp - r), (0, 0)))

    wd1, wd2 = pad_rank(wd1, wd2)
    wa1, wa2 = pad_rank(wa1, wa2)
    wg1, wg2 = pad_rank(wg1, wg2)
    row = lambda t: t.reshape(1, D)
    full = lambda shape: _resident(shape, lambda b, i: (0,) * len(shape))
    tile = pl.BlockSpec((None, tm, D), lambda b, i: (b, i, 0))
    outs = pl.pallas_call(
        _rwkv_prep_kernel,
        grid=(B, S // tm),
        in_specs=[
            tile,
            pl.BlockSpec((None, 8, D), lambda b, i: (b, jnp.maximum(i * (tm // 8) - 1, 0), 0)),
            full((6, D)), full((3, D, D)), full((1, D)), full(wd1.shape), full(wd2.shape),
            full((1, D)), full(wa1.shape), full(wa2.shape), full(wg1.shape), full(wg2.shape),
            full((1, D)), full((1, D)), full((1, D)), full((D, H)), full((H, D)),
        ],
        out_specs=[tile] * 8,
        out_shape=[jax.ShapeDtypeStruct((B, S, D), F32)] * 8,
        compiler_params=_cparams(("parallel", "parallel")),
        name="rwkv_prep",
    )(h, h, mu, w_rkv.astype(BF16), row(w0), wd1, wd2, row(a0), wa1, wa2, wg1, wg2,
      row(k_k), row(k_a), row(r_k), gs, gs.T)
    r, lw, k, v, kk, a, gg, bonus = outs
    width = pairs * LANES
    blk = pl.BlockSpec((None, C, width), lambda b, hg, c: (b, c, hg))
    y = pl.pallas_call(
        functools.partial(_rwkv_scan_kernel, pairs=pairs),
        grid=(B, D // width, S // C),
        in_specs=[blk] * 6,
        out_specs=blk,
        out_shape=jax.ShapeDtypeStruct((B, S, D), F32),
        scratch_shapes=[pltpu.VMEM((pairs, LANES, LANES), F32)],
        compiler_params=_cparams(("parallel", "parallel", "arbitrary")),
        name="rwkv_scan",
    )(r, lw, k, v, kk, a)
    tp = min(512, S)
    tile_p = pl.BlockSpec((None, tp, D), lambda b, i: (b, i, 0))
    return pl.pallas_call(
        _rwkv_post_kernel,
        grid=(B, S // tp),
        in_specs=[tile_p, tile_p, tile_p, tile_p,
                  pl.BlockSpec((None, 9, D), lambda b, i: (b, 0, 0)),
                  full((1, D)), full((1, D)), full((D, H)), full((H, D)), full((D, D))],
        out_specs=tile_p,
        out_shape=jax.ShapeDtypeStruct((B, S, D), F32),
        compiler_params=_cparams(("parallel", "parallel")),
        name="rwkv_post",
    )(x, y, bonus, gg, mods, row(gn_w), row(gn_b), gs, gs.T, w_out.astype(BF16))


def kernel(x, c, ada_w, ada_b, norm_g, ffn_w_in, ffn_w_out, final_g, rel_table, mla_w_in, mla_q_norm, mla_w_uq, mla_kv_norm, mla_w_ukv, mla_w_out, rwkv_mu, rwkv_w_rkv, rwkv_w0, rwkv_wd1, rwkv_wd2, rwkv_a0, rwkv_wa1, rwkv_wa2, rwkv_wg1, rwkv_wg2, rwkv_k_k, rwkv_k_a, rwkv_r_k, rwkv_gn_w, rwkv_gn_b, rwkv_w_out, moba_w_in, moba_w_out, ret_w_in, ret_gn_w, ret_gn_b, ret_w_out):
    depth = ada_w.shape[0]
    mods_all = _ada_mods(c, ada_w, ada_b)
    ffn_in = ffn_w_in.astype(BF16)
    ffn_out = ffn_w_out.astype(BF16)
    for i in range(depth):
        mods = mods_all[i]
        x = _ffn(x, mods, norm_g[i, 0], ffn_in[i, 0], ffn_out[i, 0], sub=0)
        kind, j = i % 4, i // 4
        g = norm_g[i, 1]
        if kind == 0:
            x = _mla_mixer(x, mods, g, mla_w_in[j], mla_q_norm[j], mla_w_uq[j], mla_kv_norm[j],
                           mla_w_ukv[j], mla_w_out[j])
        elif kind == 1:
            x = _rwkv7_mixer(x, mods, g, rwkv_mu[j], rwkv_w_rkv[j], rwkv_w0[j], rwkv_wd1[j],
                             rwkv_wd2[j], rwkv_a0[j], rwkv_wa1[j], rwkv_wa2[j], rwkv_wg1[j],
                             rwkv_wg2[j], rwkv_k_k[j], rwkv_k_a[j], rwkv_r_k[j].reshape(-1),
                             rwkv_gn_w[j], rwkv_gn_b[j], rwkv_w_out[j])
        elif kind == 2:
            x = _moba_mixer(x, mods, g, moba_w_in[j], rel_table, moba_w_out[j])
        else:
            x = _retnet_mixer(x, mods, g, ret_w_in[j], ret_gn_w[j], ret_gn_b[j], ret_w_out[j])
        x = _ffn(x, mods, norm_g[i, 2], ffn_in[i, 1], ffn_out[i, 1], sub=2)
    return _final_norm(x, final_g)
```

```python
import functools
import math

import jax
import jax.numpy as jnp
from jax import lax
from jax.experimental import pallas as pl
from jax.experimental.pallas import tpu as pltpu

F32 = jnp.float32
BF16 = jnp.bfloat16
HI = lax.Precision.HIGHEST

LANES = 128
VMEM_LIMIT_BYTES = 56 * 1024 * 1024

D_MODEL = 1024
DEPTH = 4
EPS = 1e-6
MACARON_W = 0.5
D_FF = 2816
ROPE_BASE = 10000.0
REL_BUCKETS = 32
REL_MAX_DIST = 128
MLA_HEADS = 16
MLA_Q_RANK = 384
MLA_KV_RANK = 256
MLA_NOPE = 64
MLA_ROPE = 32
MLA_V = 64
RWKV_HEAD = 64
RWKV_HEADS = D_MODEL // RWKV_HEAD
RWKV_GN_EPS = 64e-5
RWKV_CHUNK = 64
MOBA_HEADS = 16
MOBA_HEAD_DIM = 64
MOBA_BLOCK = 256
MOBA_TOPK = 3
RET_HEADS = 4
RET_DK = 256
RET_DV = 512
RET_CHUNK = 128
RET_GN_EPS = 1e-5
NEG_BIG = -1e30


def _cparams(sem, vmem=VMEM_LIMIT_BYTES):
    return pltpu.CompilerParams(dimension_semantics=sem, vmem_limit_bytes=vmem)


def _resident(block_shape, index_map):
    return pl.BlockSpec(block_shape, index_map, pipeline_mode=pl.Buffered(1))


def _dot(a, b):
    return jnp.dot(a, b, preferred_element_type=F32)


def _dot_hi(a, b):
    return jnp.dot(a, b, preferred_element_type=F32, precision=HI)


def _split(x):
    hi = x.astype(BF16)
    return hi, (x - hi.astype(F32)).astype(BF16)


def _dot3(a, b):
    return _dot(a[0], b[0]) + (_dot(a[0], b[1]) + _dot(a[1], b[0]))


def _dot3_nt(a, b):
    return _dot_nt(a[0], b[0]) + (_dot_nt(a[0], b[1]) + _dot_nt(a[1], b[0]))


def _dot_nt(a, b, precision=None):
    return lax.dot_general(a, b, (((1,), (1,)), ((), ())),
                           preferred_element_type=F32, precision=precision)


def _silu(x):
    return x * jax.nn.sigmoid(x)


def _modnorm(x, g, shift, scale):
    y = x * lax.rsqrt(jnp.mean(x * x, axis=-1, keepdims=True) + EPS)
    return (y * g) * (1 + scale) + shift


def _rms(x, g):
    return (x * lax.rsqrt(jnp.mean(x * x, axis=-1, keepdims=True) + EPS)) * g


def _ada_kernel(c_ref, w_ref, b_ref, o_ref):
    o_ref[...] = _dot_hi(_silu(c_ref[...]), w_ref[...]) + b_ref[...]


def _ada_mods(c, ada_w, ada_b):
    B, D = c.shape
    depth, _, N = ada_w.shape
    rows = 8
    cp = jnp.zeros((rows, D), F32).at[:B].set(c)
    tn = N // 8
    out = pl.pallas_call(
        _ada_kernel,
        grid=(depth, N // tn),
        in_specs=[
            pl.BlockSpec((rows, D), lambda l, n: (0, 0)),
            pl.BlockSpec((None, D, tn), lambda l, n: (l, 0, n)),
            pl.BlockSpec((None, 1, tn), lambda l, n: (l, 0, n)),
        ],
        out_specs=pl.BlockSpec((None, rows, tn), lambda l, n: (l, 0, n)),
        out_shape=jax.ShapeDtypeStruct((depth, rows, N), F32),
        compiler_params=_cparams(("parallel", "parallel")),
        name="ada_mods",
    )(cp, ada_w, ada_b.reshape(depth, 1, N))
    return out[:, :B].reshape(depth, B, 9, D)


def _ffn_kernel(x_ref, m_ref, g_ref, wg_ref, wu_ref, wo_ref, o_ref, *, sub, tf):
    x = x_ref[...]
    shift = m_ref[3 * sub:3 * sub + 1, :]
    scale = m_ref[3 * sub + 1:3 * sub + 2, :]
    gate = m_ref[3 * sub + 2:3 * sub + 3, :]
    h = _modnorm(x, g_ref[...], shift, scale).astype(BF16)
    acc = jnp.zeros(x.shape, F32)
    for j in range(wg_ref.shape[1] // tf):
        gt = _dot(h, wg_ref[:, j * tf:(j + 1) * tf])
        up = _dot(h, wu_ref[:, j * tf:(j + 1) * tf])
        a = (_silu(gt) * up).astype(BF16)
        acc = acc + _dot(a, wo_ref[j * tf:(j + 1) * tf, :])
    o_ref[...] = x + (MACARON_W * gate) * acc


def _ffn(x, mods, g, w_in, w_out, sub, tm=512):
    B, S, D = x.shape
    dff = w_out.shape[0]
    tm = min(tm, S)
    tf = dff // 2
    return pl.pallas_call(
        functools.partial(_ffn_kernel, sub=sub, tf=tf),
        grid=(B, S // tm),
        in_specs=[
            pl.BlockSpec((None, tm, D), lambda b, i: (b, i, 0)),
            pl.BlockSpec((None, 9, D), lambda b, i: (b, 0, 0)),
            pl.BlockSpec((1, D), lambda b, i: (0, 0)),
            _resident((D, dff), lambda b, i: (0, 0)),
            _resident((D, dff), lambda b, i: (0, 1)),
            _resident((dff, D), lambda b, i: (0, 0)),
        ],
        out_specs=pl.BlockSpec((None, tm, D), lambda b, i: (b, i, 0)),
        out_shape=jax.ShapeDtypeStruct((B, S, D), F32),
        compiler_params=_cparams(("parallel", "parallel")),
        name="ffn",
    )(x, mods, g.reshape(1, D), w_in, w_in, w_out)


def _proj_kernel(x_ref, m_ref, g_ref, w_ref, *o_and_scr, sub):
    o_refs, h_scr = o_and_scr[:-1], o_and_scr[-1]

    @pl.when(pl.program_id(2) == 0)
    def _():
        h_scr[...] = _modnorm(x_ref[...], g_ref[...], m_ref[3 * sub:3 * sub + 1, :],
                              m_ref[3 * sub + 1:3 * sub + 2, :]).astype(BF16)

    y = _dot(h_scr[...], w_ref[...])
    for o_ref in o_refs:
        o_ref[...] = y.astype(o_ref.dtype)


def _proj(x, mods, g, w, out_dtypes, sub=1, tm=1024, tn=1024):
    B, S, D = x.shape
    N = w.shape[1]
    tm = min(tm, S)
    tn = tn if N % tn == 0 else N
    outs = pl.pallas_call(
        functools.partial(_proj_kernel, sub=sub),
        grid=(B, S // tm, N // tn),
        in_specs=[
            pl.BlockSpec((None, tm, D), lambda b, i, n: (b, i, 0)),
            pl.BlockSpec((None, 9, D), lambda b, i, n: (b, 0, 0)),
            pl.BlockSpec((1, D), lambda b, i, n: (0, 0)),
            pl.BlockSpec((D, tn), lambda b, i, n: (0, n)),
        ],
        out_specs=[pl.BlockSpec((None, tm, tn), lambda b, i, n: (b, i, n)) for _ in out_dtypes],
        out_shape=[jax.ShapeDtypeStruct((B, S, N), dt) for dt in out_dtypes],
        scratch_shapes=[pltpu.VMEM((tm, D), BF16)],
        compiler_params=_cparams(("parallel", "parallel", "arbitrary")),
        name="modnorm_proj",
    )(x, mods, g.reshape(1, D), w)
    return outs


def _proj_t_kernel(x_ref, m_ref, g_ref, wt_ref, o_ref, *, sub):
    h = _modnorm(x_ref[...], g_ref[...], m_ref[3 * sub:3 * sub + 1, :],
                 m_ref[3 * sub + 1:3 * sub + 2, :]).astype(BF16)
    o_ref[...] = _dot_nt(wt_ref[...], h).astype(o_ref.dtype)


def _proj_t(x, mods, g, wt, out_dtype, sub=1, tm=1024):
    B, S, D = x.shape
    N = wt.shape[0]
    tm = min(tm, S)
    return pl.pallas_call(
        functools.partial(_proj_t_kernel, sub=sub),
        grid=(B, S // tm),
        in_specs=[
            pl.BlockSpec((None, tm, D), lambda b, i: (b, i, 0)),
            pl.BlockSpec((None, 9, D), lambda b, i: (b, 0, 0)),
            pl.BlockSpec((1, D), lambda b, i: (0, 0)),
            _resident((N, D), lambda b, i: (0, 0)),
        ],
        out_specs=pl.BlockSpec((None, N, tm), lambda b, i: (b, 0, i)),
        out_shape=jax.ShapeDtypeStruct((B, N, S), out_dtype),
        compiler_params=_cparams(("parallel", "parallel")),
        name="modnorm_proj_t",
    )(x, mods, g.reshape(1, D), wt)


def _norm_kernel(x_ref, m_ref, g_ref, o_ref, *, sub):
    o_ref[...] = _modnorm(x_ref[...], g_ref[...], m_ref[3 * sub:3 * sub + 1, :],
                          m_ref[3 * sub + 1:3 * sub + 2, :])


def _modnorm_call(x, mods, g, sub=1, tm=1024):
    B, S, D = x.shape
    tm = min(tm, S)
    return pl.pallas_call(
        functools.partial(_norm_kernel, sub=sub),
        grid=(B, S // tm),
        in_specs=[
            pl.BlockSpec((None, tm, D), lambda b, i: (b, i, 0)),
            pl.BlockSpec((None, 9, D), lambda b, i: (b, 0, 0)),
            pl.BlockSpec((1, D), lambda b, i: (0, 0)),
        ],
        out_specs=pl.BlockSpec((None, tm, D), lambda b, i: (b, i, 0)),
        out_shape=jax.ShapeDtypeStruct((B, S, D), F32),
        compiler_params=_cparams(("parallel", "parallel")),
        name="modnorm",
    )(x, mods, g.reshape(1, D))


def _out_kernel(x_ref, y_ref, m_ref, w_ref, o_ref):
    o_ref[...] = x_ref[...] + m_ref[5:6, :] * _dot(y_ref[...], w_ref[...])


def _out_proj(x, y, mods, w, tm=512):
    B, S, D = x.shape
    K = w.shape[0]
    tm = min(tm, S)
    return pl.pallas_call(
        _out_kernel,
        grid=(B, S // tm),
        in_specs=[
            pl.BlockSpec((None, tm, D), lambda b, i: (b, i, 0)),
            pl.BlockSpec((None, tm, K), lambda b, i: (b, i, 0)),
            pl.BlockSpec((None, 9, D), lambda b, i: (b, 0, 0)),
            _resident((K, D), lambda b, i: (0, 0)),
        ],
        out_specs=pl.BlockSpec((None, tm, D), lambda b, i: (b, i, 0)),
        out_shape=jax.ShapeDtypeStruct((B, S, D), F32),
        compiler_params=_cparams(("parallel", "parallel")),
        name="out_proj",
    )(x, y, mods, w)


def _final_kernel(x_ref, g_ref, o_ref):
    o_ref[...] = _rms(x_ref[...], g_ref[...])


def _final_norm(x, g, tm=1024):
    B, S, D = x.shape
    tm = min(tm, S)
    return pl.pallas_call(
        _final_kernel,
        grid=(B, S // tm),
        in_specs=[pl.BlockSpec((None, tm, D), lambda b, i: (b, i, 0)),
                  pl.BlockSpec((1, D), lambda b, i: (0, 0))],
        out_specs=pl.BlockSpec((None, tm, D), lambda b, i: (b, i, 0)),
        out_shape=jax.ShapeDtypeStruct((B, S, D), F32),
        compiler_params=_cparams(("parallel", "parallel")),
        name="final_norm",
    )(x, g.reshape(1, D))


def _rope_tables(seq, dim):
    inv = ROPE_BASE ** (-jnp.arange(0, dim, 2, dtype=F32) / dim)
    ang = jnp.arange(seq, dtype=F32)[:, None] * inv[None, :]
    return jnp.cos(ang), jnp.sin(ang)


def _mla_prep_kernel(p_ref, qn_ref, kvn_ref, wq_ref, wqs_ref, wk_ref, wvt_ref, c_ref, s_ref,
                     q_ref, k_ref, vt_ref):
    p = p_ref[...]
    cq = _rms(p[:, :MLA_Q_RANK], qn_ref[...]).astype(BF16)
    ckv = _rms(p[:, MLA_Q_RANK:MLA_Q_RANK + MLA_KV_RANK], kvn_ref[...]).astype(BF16)
    o = MLA_Q_RANK + MLA_KV_RANK
    cos = c_ref[...]
    sin = s_ref[...]
    k_rope = p[:, o:o + LANES] * cos + p[:, o + LANES:o + 2 * LANES] * sin
    q1 = _dot(cq, wq_ref[...])
    q2 = _dot(cq, wqs_ref[...])
    kn = _dot(ckv, wk_ref[...])
    scale = (MLA_NOPE + MLA_ROPE) ** -0.5
    for h in range(MLA_HEADS):
        sl = slice(h * LANES, (h + 1) * LANES)
        q_ref[:, sl] = ((q1[:, sl] * cos + q2[:, sl] * sin) * scale).astype(BF16)
        k_ref[:, sl] = (kn[:, sl] + k_rope).astype(BF16)
    vt_ref[...] = _dot_nt(wvt_ref[...], ckv).astype(BF16)


def _flash_update(s_t, vt, m_scr, l_scr, acc_scr, h):
    m_old = m_scr[h]
    m_new = jnp.maximum(m_old, jnp.max(s_t, axis=0, keepdims=True))
    alpha = jnp.exp(m_old - m_new)
    p_t = jnp.exp(s_t - m_new)
    l_scr[h] = alpha * l_scr[h] + jnp.sum(p_t, axis=0, keepdims=True)
    acc_scr[h] = alpha * acc_scr[h] + _dot(vt, p_t.astype(BF16))
    m_scr[h] = m_new


def _flash_init(m_scr, l_scr, acc_scr):
    m_scr[...] = jnp.full(m_scr.shape, -jnp.inf, F32)
    l_scr[...] = jnp.zeros(l_scr.shape, F32)
    acc_scr[...] = jnp.zeros(acc_scr.shape, F32)


def _flash_finish(o_ref, l_scr, acc_scr):
    o_t = jnp.concatenate([acc_scr[h] / l_scr[h] for h in range(acc_scr.shape[0])], axis=0)
    o_ref[...] = o_t.T.astype(o_ref.dtype)


def _mla_attn_kernel(q_ref, k_ref, vt_ref, o_ref, s_scr, m_scr, l_scr, acc_scr, *, t):
    i = pl.program_id(2)
    nh = acc_scr.shape[0]
    _flash_init(m_scr, l_scr, acc_scr)
    q = q_ref[...]

    def scores(j, slot):
        kb = k_ref[pl.ds(pl.multiple_of(j * t, t), t), :]
        for h in range(nh):
            s_scr[slot, h] = _dot_nt(kb[:, h * LANES:(h + 1) * LANES], q[:, h * LANES:(h + 1) * LANES])

    def consume(j, slot, diagonal):
        vtb = vt_ref[:, pl.ds(pl.multiple_of(j * t, t), t)]
        for h in range(nh):
            s_t = s_scr[slot, h]
            if diagonal:
                key = lax.broadcasted_iota(jnp.int32, s_t.shape, 0)
                qry = lax.broadcasted_iota(jnp.int32, s_t.shape, 1)
                s_t = jnp.where(key <= qry, s_t, -jnp.inf)
            _flash_update(s_t, vtb[h * MLA_V:(h + 1) * MLA_V, :], m_scr, l_scr, acc_scr, h)

    scores(0, 0)

    def body(p, carry):
        j = 2 * p
        scores(j + 1, 1)
        consume(j, 0, False)
        scores(j + 2, 0)
        consume(j + 1, 1, False)
        return carry

    lax.fori_loop(0, i // 2, body, 0)

    @pl.when(i % 2 == 0)
    def _():
        consume(i, 0, True)

    @pl.when(i % 2 == 1)
    def _():
        scores(i, 1)
        consume(i - 1, 0, False)
        consume(i, 1, True)

    _flash_finish(o_ref, l_scr, acc_scr)


def _mla_weights(w_in, w_uq, w_ukv):
    H, dn, dr, dv = MLA_HEADS, MLA_NOPE, MLA_ROPE, MLA_V
    D = w_in.shape[0]
    o = MLA_Q_RANK + MLA_KV_RANK
    kr = w_in[:, o:]
    z64 = jnp.zeros((D, dn), F32)
    z32 = jnp.zeros((D, LANES - dn - dr), F32)
    half = dr // 2
    w_in_p = jnp.concatenate(
        [w_in[:, :o], z64, kr, z32, z64, -kr[:, half:], kr[:, :half], z32], axis=1)
    wq = w_uq.reshape(MLA_Q_RANK, H, dn + dr)
    zq = jnp.zeros((MLA_Q_RANK, H, LANES - dn - dr), F32)
    wq_p = jnp.concatenate([wq, zq], axis=2).reshape(MLA_Q_RANK, H * LANES)
    wqs_p = jnp.concatenate(
        [jnp.zeros((MLA_Q_RANK, H, dn), F32), -wq[:, :, dn + half:], wq[:, :, dn:dn + half], zq],
        axis=2).reshape(MLA_Q_RANK, H * LANES)
    wkv = w_ukv.reshape(MLA_KV_RANK, H, dn + dv)
    wk_p = jnp.concatenate(
        [wkv[:, :, :dn], jnp.zeros((MLA_KV_RANK, H, LANES - dn), F32)], axis=2).reshape(MLA_KV_RANK, H * LANES)
    wvt_p = wkv[:, :, dn:].reshape(MLA_KV_RANK, H * dv).T
    return (w_in_p.astype(BF16), wq_p.astype(BF16), wqs_p.astype(BF16),
            wk_p.astype(BF16), wvt_p.astype(BF16))


def _mla_mixer(x, mods, g, w_in, q_norm, w_uq, kv_norm, w_ukv, w_out, tm=512, t=256, nh=8):
    B, S, D = x.shape
    H = MLA_HEADS
    tm = min(tm, S)
    t = min(t, S)
    w_in_p, wq_p, wqs_p, wk_p, wvt_p = _mla_weights(w_in, w_uq, w_ukv)
    (p,) = _proj(x, mods, g, w_in_p, [F32])
    cos, sin = _rope_tables(S, MLA_ROPE)
    one = jnp.ones((S, MLA_NOPE), F32)
    pad = LANES - MLA_NOPE - MLA_ROPE
    tab_c = jnp.concatenate([one, cos, cos, jnp.ones((S, pad), F32)], axis=1)
    tab_s = jnp.concatenate([0 * one, sin, sin, jnp.zeros((S, pad), F32)], axis=1)
    NP = p.shape[-1]
    q, k, vt = pl.pallas_call(
        _mla_prep_kernel,
        grid=(B, S // tm),
        in_specs=[
            pl.BlockSpec((None, tm, NP), lambda b, i: (b, i, 0)),
            pl.BlockSpec((1, MLA_Q_RANK), lambda b, i: (0, 0)),
            pl.BlockSpec((1, MLA_KV_RANK), lambda b, i: (0, 0)),
            _resident(wq_p.shape, lambda b, i: (0, 0)),
            _resident(wqs_p.shape, lambda b, i: (0, 0)),
            _resident(wk_p.shape, lambda b, i: (0, 0)),
            _resident(wvt_p.shape, lambda b, i: (0, 0)),
            pl.BlockSpec((tm, LANES), lambda b, i: (i, 0)),
            pl.BlockSpec((tm, LANES), lambda b, i: (i, 0)),
        ],
        out_specs=[
            pl.BlockSpec((None, tm, H * LANES), lambda b, i: (b, i, 0)),
            pl.BlockSpec((None, tm, H * LANES), lambda b, i: (b, i, 0)),
            pl.BlockSpec((None, H * MLA_V, tm), lambda b, i: (b, 0, i)),
        ],
        out_shape=[
            jax.ShapeDtypeStruct((B, S, H * LANES), BF16),
            jax.ShapeDtypeStruct((B, S, H * LANES), BF16),
            jax.ShapeDtypeStruct((B, H * MLA_V, S), BF16),
        ],
        compiler_params=_cparams(("parallel", "parallel")),
        name="mla_prep",
    )(p, q_norm.reshape(1, -1), kv_norm.reshape(1, -1), wq_p, wqs_p, wk_p, wvt_p, tab_c, tab_s)
    o = pl.pallas_call(
        functools.partial(_mla_attn_kernel, t=t),
        grid=(B, H // nh, S // t),
        in_specs=[
            pl.BlockSpec((None, t, nh * LANES), lambda b, hg, i: (b, i, hg)),
            _resident((None, S, nh * LANES), lambda b, hg, i: (b, 0, hg)),
            _resident((None, nh * MLA_V, S), lambda b, hg, i: (b, hg, 0)),
        ],
        out_specs=pl.BlockSpec((None, t, nh * MLA_V), lambda b, hg, i: (b, i, hg)),
        out_shape=jax.ShapeDtypeStruct((B, S, H * MLA_V), BF16),
        scratch_shapes=[pltpu.VMEM((2, nh, t, t), F32),
                        pltpu.VMEM((nh, 1, t), F32), pltpu.VMEM((nh, 1, t), F32),
                        pltpu.VMEM((nh, MLA_V, t), F32)],
        compiler_params=_cparams(("parallel", "parallel", "arbitrary")),
        name="mla_attn",
    )(q, k, vt)
    return _out_proj(x, o, mods, w_out.astype(BF16))


def _moba_kmean_kernel(k_ref, o_ref):
    o_ref[...] = jnp.mean(k_ref[...], axis=0, keepdims=True)


def _moba_select_kernel(q_ref, km_ref, o_ref, *, nb):
    H, dh, tq = MOBA_HEADS, MOBA_HEAD_DIM, MOBA_BLOCK
    own = pl.program_id(1)
    km = km_ref[...]
    kbd = jnp.concatenate([km] * H, axis=0)
    assert nb & (nb - 1) == 0 and dh & (dh - 1) == 0
    row_head = lax.shift_right_logical(lax.broadcasted_iota(jnp.int32, kbd.shape, 0), nb.bit_length() - 1)
    lane_head = lax.shift_right_logical(lax.broadcasted_iota(jnp.int32, kbd.shape, 1), dh.bit_length() - 1)
    kbd = jnp.where(row_head == lane_head, kbd, 0.0)
    gate = _dot_nt(kbd, q_ref[...], precision=HI)
    g = gate.reshape(H, nb, tq)
    n_iota = lax.broadcasted_iota(jnp.int32, g.shape, 1)
    g = jnp.where(n_iota < own, g, -jnp.inf)
    sel = jnp.zeros(g.shape, jnp.bool_)
    for _ in range(min(MOBA_TOPK, nb)):
        m = jnp.max(g, axis=1, keepdims=True)
        first = jnp.min(jnp.where(g == m, n_iota, nb), axis=1, keepdims=True)
        pick = n_iota == first
        sel = sel | (pick & (m > -jnp.inf))
        g = jnp.where(pick, -jnp.inf, g)
    bias = jnp.where(sel | (n_iota == own), 0.0, NEG_BIG)
    if nb < LANES:
        bias = jnp.concatenate([bias, jnp.zeros((H, LANES - nb, tq), F32)], axis=1)
    o_ref[...] = bias.reshape(H * LANES, tq).T.astype(o_ref.dtype)


def _moba_bias_kernel(tbl_ref, o_ref):
    h = pl.program_id(0)
    blk = MOBA_BLOCK
    key = lax.broadcasted_iota(jnp.int32, (blk, blk), 0)
    qry = lax.broadcasted_iota(jnp.int32, (blk, blk), 1)
    max_exact = REL_BUCKETS // 2
    for t in range(2):
        n = jnp.maximum(qry - key + blk * t, 0)
        nf = jnp.maximum(n, max_exact).astype(F32)
        large = max_exact + (jnp.log(nf / max_exact) / math.log(REL_MAX_DIST / max_exact)
                             * (REL_BUCKETS - max_exact)).astype(jnp.int32)
        large = jnp.minimum(large, REL_BUCKETS - 1)
        bucket = jnp.where(n < max_exact, n, large)
        acc = jnp.zeros((blk, blk), F32)
        for b in range(REL_BUCKETS):
            acc = jnp.where(bucket == b, tbl_ref[b, h], acc)
        o_ref[t] = acc


def _moba_attn_kernel(tbl_ref, q_ref, k_ref, vt_ref, sel_ref, bias_ref, o_ref,
                      s_scr, m_scr, l_scr, acc_scr):
    hg = pl.program_id(1)
    i = pl.program_id(2)
    t = MOBA_BLOCK
    dh = MOBA_HEAD_DIM
    nh = acc_scr.shape[0]
    _flash_init(m_scr, l_scr, acc_scr)
    lane = lax.broadcasted_iota(jnp.int32, (t, LANES), 1)
    scale = dh ** -0.5
    qa = []
    for h in range(nh):
        q = q_ref[:, (h // 2) * LANES:(h // 2 + 1) * LANES]
        q = jnp.where((lane < dh) == (h % 2 == 0), q, 0) * scale
        qa.append(jnp.concatenate([q, sel_ref[:, h * LANES:(h + 1) * LANES]], axis=1))

    def scores(j, slot):
        kb = k_ref[pl.ds(pl.multiple_of(j * t, t), t), :]
        onehot = jnp.where(lane == j, 1.0, 0.0).astype(BF16)
        for h in range(nh):
            ka = jnp.concatenate([kb[:, (h // 2) * LANES:(h // 2 + 1) * LANES], onehot], axis=1)
            s_scr[slot, h] = _dot_nt(ka, qa[h])

    FAR, PREV, OWN = 0, 1, 2

    def consume(j, slot, kind):
        vtb = vt_ref[:, pl.ds(pl.multiple_of(j * t, t), t)]
        for h in range(nh):
            if kind == FAR:
                s_t = s_scr[slot, h] + tbl_ref[REL_BUCKETS - 1, nh * hg + h]
            elif kind == PREV:
                s_t = s_scr[slot, h] + bias_ref[h, 1]
            else:
                key = lax.broadcasted_iota(jnp.int32, (t, t), 0)
                qry = lax.broadcasted_iota(jnp.int32, (t, t), 1)
                s_t = jnp.where(key <= qry, s_scr[slot, h] + bias_ref[h, 0], -jnp.inf)
            _flash_update(s_t, vtb[h * dh:(h + 1) * dh, :], m_scr, l_scr, acc_scr, h)

    scores(0, 0)
    n_far = jnp.maximum(i - 1, 0)

    def body(p, carry):
        j = 2 * p
        scores(j + 1, 1)
        consume(j, 0, FAR)
        scores(j + 2, 0)
        consume(j + 1, 1, FAR)
        return carry

    lax.fori_loop(0, n_far // 2, body, 0)

    @pl.when(i == 0)
    def _():
        consume(i, 0, OWN)

    @pl.when(i % 2 == 1)
    def _():
        scores(i, 1)
        consume(i - 1, 0, PREV)
        consume(i, 1, OWN)

    @pl.when((i % 2 == 0) & (i > 0))
    def _():
        scores(i - 1, 1)
        consume(i - 2, 0, FAR)
        scores(i, 0)
        consume(i - 1, 1, PREV)
        consume(i, 0, OWN)

    _flash_finish(o_ref, l_scr, acc_scr)


def _moba_mixer(x, mods, g, w_in, rel_table, w_out, nh=8):
    B, S, D = x.shape
    H, t = MOBA_HEADS, MOBA_BLOCK
    nb = S // t
    w_in16 = w_in.astype(BF16)
    qkv, qkv16 = _proj(x, mods, g, w_in16[:, :2 * D], [F32, BF16])
    vt = _proj_t(x, mods, g, w_in16[:, 2 * D:].T, BF16)
    kmean = pl.pallas_call(
        _moba_kmean_kernel,
        grid=(B, nb),
        in_specs=[pl.BlockSpec((None, t, D), lambda b, n: (b, n, 1))],
        out_specs=pl.BlockSpec((None, None, 1, D), lambda b, n: (b, n, 0, 0)),
        out_shape=jax.ShapeDtypeStruct((B, nb, 1, D), F32),
        compiler_params=_cparams(("parallel", "parallel")),
        name="moba_kmean",
    )(qkv).reshape(B, nb, D)
    sel = pl.pallas_call(
        functools.partial(_moba_select_kernel, nb=nb),
        grid=(B, nb),
        in_specs=[pl.BlockSpec((None, t, D), lambda b, i: (b, i, 0)),
                  pl.BlockSpec((None, nb, D), lambda b, i: (b, 0, 0))],
        out_specs=pl.BlockSpec((None, t, H * LANES), lambda b, i: (b, i, 0)),
        out_shape=jax.ShapeDtypeStruct((B, S, H * LANES), BF16),
        compiler_params=_cparams(("parallel", "parallel")),
        name="moba_select",
    )(qkv, kmean)
    bias = pl.pallas_call(
        _moba_bias_kernel,
        grid=(H,),
        in_specs=[pl.BlockSpec(memory_space=pltpu.SMEM)],
        out_specs=pl.BlockSpec((None, 2, t, t), lambda h: (h, 0, 0, 0)),
        out_shape=jax.ShapeDtypeStruct((H, 2, t, t), F32),
        compiler_params=_cparams(("parallel",)),
        name="moba_bias",
    )(rel_table)
    ng = H // nh
    w = nh * MOBA_HEAD_DIM
    o = pl.pallas_call(
        _moba_attn_kernel,
        grid=(B, ng, nb),
        in_specs=[
            pl.BlockSpec(memory_space=pltpu.SMEM),
            pl.BlockSpec((None, t, w), lambda b, hg, i: (b, i, hg)),
            _resident((None, S, w), lambda b, hg, i: (b, 0, ng + hg)),
            _resident((None, w, S), lambda b, hg, i: (b, hg, 0)),
            pl.BlockSpec((None, t, nh * LANES), lambda b, hg, i: (b, i, hg)),
            pl.BlockSpec((nh, 2, t, t), lambda b, hg, i: (hg, 0, 0, 0)),
        ],
        out_specs=pl.BlockSpec((None, t, w), lambda b, hg, i: (b, i, hg)),
        out_shape=jax.ShapeDtypeStruct((B, S, D), BF16),
        scratch_shapes=[pltpu.VMEM((2, nh, t, t), F32),
                        pltpu.VMEM((nh, 1, t), F32), pltpu.VMEM((nh, 1, t), F32),
                        pltpu.VMEM((nh, MOBA_HEAD_DIM, t), F32)],
        compiler_params=_cparams(("parallel", "parallel", "arbitrary")),
        name="moba_attn",
    )(rel_table, qkv16, qkv16, vt, sel, bias)
    return _out_proj(x, o, mods, w_out.astype(BF16))


def _ret_kernel(gam_ref, q_ref, k_ref, v_ref, g_ref, cos_ref, sin_ref, dm_ref, zeta_ref, xi_ref,
                gnw_ref, gnb_ref, o_ref, r_scr):
    h = pl.program_id(1)

    @pl.when(pl.program_id(2) == 0)
    def _():
        r_scr[...] = jnp.zeros(r_scr.shape, F32)

    cos = cos_ref[...]
    sin = sin_ref[...]
    half = RET_DK // 2

    def rope(t):
        t1, t2 = t[:, :half], t[:, half:]
        return jnp.concatenate([t1 * cos - t2 * sin, t1 * sin + t2 * cos], axis=1)

    q = rope(q_ref[...])
    k = rope(k_ref[...] * RET_DK ** -0.5)
    q16 = q.astype(BF16)
    v16 = v_ref[...].astype(BF16)
    R = r_scr[...]
    inner = _dot_nt(q16, k.astype(BF16)) * dm_ref[...]
    o = _dot(inner.astype(BF16), v16) + _dot(q16, R.astype(BF16)) * xi_ref[...]
    r_scr[...] = gam_ref[h] * R + _dot((k * zeta_ref[...]).T.astype(BF16), v16)
    mu = jnp.mean(o, axis=-1, keepdims=True)
    d = o - mu
    var = jnp.mean(d * d, axis=-1, keepdims=True)
    yn = (d * lax.rsqrt(var + RET_GN_EPS)) * gnw_ref[...] + gnb_ref[...]
    o_ref[...] = (_silu(g_ref[...]) * yn).astype(o_ref.dtype)


def _retnet_mixer(x, mods, g, w_in, gn_w, gn_b, w_out):
    B, S, D = x.shape
    H, DK, DV, C = RET_HEADS, RET_DK, RET_DV, RET_CHUNK
    C = min(C, S)
    (p,) = _proj(x, mods, g, w_in.astype(BF16), [F32])
    cos, sin = _rope_tables(S, DK)
    log_gamma = jnp.log1p(-jnp.exp2(-5.0 - jnp.arange(H, dtype=F32)))
    idx = jnp.arange(C, dtype=F32)
    diff = idx[:, None] - idx[None, :]
    dmask = jnp.where(diff >= 0, jnp.exp(jnp.maximum(diff, 0.0)[None] * log_gamma[:, None, None]), 0.0)
    zeta = jnp.exp((C - 1 - idx)[None, :] * log_gamma[:, None])[:, :, None]
    xi = jnp.exp((idx + 1)[None, :] * log_gamma[:, None])[:, :, None]
    gamma_c = jnp.exp(C * log_gamma)
    kq, kk, kv, kg = 0, D // DK, 2 * D // DV, 4 * D // DV
    y = pl.pallas_call(
        _ret_kernel,
        grid=(B, H, S // C),
        in_specs=[
            pl.BlockSpec(memory_space=pltpu.SMEM),
            pl.BlockSpec((None, C, DK), lambda b, h, c: (b, c, kq + h)),
            pl.BlockSpec((None, C, DK), lambda b, h, c: (b, c, kk + h)),
            pl.BlockSpec((None, C, DV), lambda b, h, c: (b, c, kv + h)),
            pl.BlockSpec((None, C, DV), lambda b, h, c: (b, c, kg + h)),
            pl.BlockSpec((C, DK // 2), lambda b, h, c: (c, 0)),
            pl.BlockSpec((C, DK // 2), lambda b, h, c: (c, 0)),
            pl.BlockSpec((None, C, C), lambda b, h, c: (h, 0, 0)),
            pl.BlockSpec((None, C, 1), lambda b, h, c: (h, 0, 0)),
            pl.BlockSpec((None, C, 1), lambda b, h, c: (h, 0, 0)),
            pl.BlockSpec((1, DV), lambda b, h, c: (0, h)),
            pl.BlockSpec((1, DV), lambda b, h, c: (0, h)),
        ],
        out_specs=pl.BlockSpec((None, C, DV), lambda b, h, c: (b, c, h)),
        out_shape=jax.ShapeDtypeStruct((B, S, H * DV), BF16),
        scratch_shapes=[pltpu.VMEM((DK, DV), F32)],
        compiler_params=_cparams(("parallel", "parallel", "arbitrary")),
        name="retention",
    )(gamma_c, p, p, p, p, cos, sin, dmask, zeta, xi, gn_w.reshape(1, -1), gn_b.reshape(1, -1))
    return _out_proj(x, y, mods, w_out.astype(BF16))


def _head_sum(t, gs_ref, gst_ref):
    def times_indicator(x, m):
        hi, lo = _split(x)
        return _dot(hi, m) + _dot(lo, m)

    return times_indicator(times_indicator(t, gs_ref[...]), gst_ref[...])


def _rwkv_prep_kernel(h_ref, hp_ref, mu_ref, wrkv_ref, w0_ref, wd1_ref, wd2_ref, a0_ref, wa1_ref,
                      wa2_ref, wg1_ref, wg2_ref, kk_ref, ka_ref, rk_ref, gs_ref, gst_ref,
                      r_o, lw_o, k_o, v_o, kk_o, a_o, g_o, bonus_o):
    i = pl.program_id(1)
    h = h_ref[...]
    tm = h.shape[0]
    prev_tail = jnp.where(i > 0, hp_ref[7:8, :], 0.0)
    rolled = pltpu.roll(h, 1, axis=0)
    row = lax.broadcasted_iota(jnp.int32, h.shape, 0)
    xx = jnp.where(row == 0, prev_tail, rolled) - h
    xs = [h + xx * mu_ref[n:n + 1, :] for n in range(6)]
    r = _dot(xs[0].astype(BF16), wrkv_ref[0])
    k = _dot(xs[1].astype(BF16), wrkv_ref[1])
    v = _dot(xs[2].astype(BF16), wrkv_ref[2])
    def low_rank(t, w1_ref, w2_ref, act):
        return _dot(act(_dot(t.astype(BF16), w1_ref[...])).astype(BF16), w2_ref[...])

    z = -(w0_ref[...] + low_rank(xs[3], wd1_ref, wd2_ref, jnp.tanh))
    softplus = jnp.maximum(z, 0.0) + jnp.log(1.0 + jnp.exp(-jnp.abs(z)))
    w = -softplus - 0.5
    lw_o[...] = -jnp.exp(w)
    a = jax.nn.sigmoid(a0_ref[...] + low_rank(xs[4], wa1_ref, wa2_ref, lambda t: t))
    g_o[...] = low_rank(xs[5], wg1_ref, wg2_ref, jax.nn.sigmoid)
    kk = k * kk_ref[...]
    norm = jnp.sqrt(_head_sum(kk * kk, gs_ref, gst_ref))
    kk_o[...] = kk / jnp.maximum(norm, 1e-12)
    k_mod = k * (1 + (a - 1) * ka_ref[...])
    r_o[...] = r
    k_o[...] = k_mod
    v_o[...] = v
    a_o[...] = a
    bonus_o[...] = _head_sum(r * k_mod * rk_ref[...], gs_ref, gst_ref) * v


def _rwkv_scan_kernel(r_ref, lw_ref, k_ref, v_ref, kk_ref, a_ref, y_ref, st_scr, *, pairs):
    C = RWKV_CHUNK
    P = 2 * C

    @pl.when(pl.program_id(2) == 0)
    def _():
        st_scr[...] = jnp.zeros(st_scr.shape, F32)

    tri = (lax.broadcasted_iota(jnp.int32, (C, C), 0) >= lax.broadcasted_iota(jnp.int32, (C, C), 1))
    tri = jnp.where(tri, 1.0, 0.0).astype(BF16)
    lane = lax.broadcasted_iota(jnp.int32, (C, LANES), 1)
    first = lane < RWKV_HEAD
    rt = lax.broadcasted_iota(jnp.int32, (P, P), 0) & (C - 1)
    ct = lax.broadcasted_iota(jnp.int32, (P, P), 1) & (C - 1)
    strict = rt > ct
    incl = rt >= ct

    def stack(t):
        return jnp.concatenate([jnp.where(first, t, 0.0), jnp.where(first, 0.0, t)], axis=0)

    R = range(pairs)
    sls = [slice(p * LANES, (p + 1) * LANES) for p in R]
    lw = [lw_ref[:, sl] for sl in sls]
    lw_hi = [x.astype(BF16) for x in lw]
    lw_r1 = [lw[p] - lw_hi[p].astype(F32) for p in R]
    lw_mid = [x.astype(BF16) for x in lw_r1]
    lw_lo = [(lw_r1[p] - lw_mid[p].astype(F32)).astype(BF16) for p in R]
    cum = [_dot(tri, lw_hi[p]) + (_dot(tri, lw_mid[p]) + _dot(tri, lw_lo[p])) for p in R]
    p_inc = [jnp.exp(c) for c in cum]
    p_inv = [jnp.exp(-c) for c in cum]
    kk = [kk_ref[:, sl] for sl in sls]
    a_s = [stack(-kk[p] * jnp.exp(cum[p] - lw[p])).astype(BF16) for p in R]
    bk_f = [jnp.concatenate([stack(kk[p] * a_ref[:, sls[p]] * p_inv[p]),
                             stack(k_ref[:, sls[p]] * p_inv[p])], axis=0) for p in R]
    bk = [x.astype(BF16) for x in bk_f]
    bk_t = [x.T.astype(BF16) for x in bk_f]
    r_s = [stack(r_ref[:, sls[p]] * p_inc[p]).astype(BF16) for p in R]
    v_s = [stack(v_ref[:, sl]).astype(BF16) for sl in sls]
    st = [st_scr[p] for p in R]
    st_s = [x.astype(BF16) for x in st]
    gram = [_dot_nt(jnp.concatenate([a_s[p], r_s[p]], axis=0), bk[p]) for p in R]
    lp = [jnp.where(strict, g[:P, :P], 0.0).astype(BF16) for g in gram]
    l_ak = [jnp.where(strict, g[:P, P:], 0.0).astype(BF16) for g in gram]
    a_rb = [jnp.where(incl, g[P:, :P], 0.0).astype(BF16) for g in gram]
    a_rk = [jnp.where(incl, g[P:, P:], 0.0).astype(BF16) for g in gram]
    u = [_dot(jnp.concatenate([a_s[p], l_ak[p]], axis=1),
              jnp.concatenate([st_s[p], v_s[p]], axis=0)) for p in R]
    steps = int(math.log2(C))
    for step in range(steps):
        u_s = [x.astype(BF16) for x in u]
        if step < steps - 1:
            prod = [_dot(lp[p], jnp.concatenate([u_s[p], lp[p]], axis=1)) for p in R]
            u = [u[p] + prod[p][:, :LANES] for p in R]
            lp = [x[:, LANES:].astype(BF16) for x in prod]
        else:
            u = [u[p] + _dot(lp[p], u_s[p]) for p in R]
    u_s = [x.astype(BF16) for x in u]
    o_s = [_dot(jnp.concatenate([r_s[p], a_rb[p], a_rk[p]], axis=1),
                jnp.concatenate([st_s[p], u_s[p], v_s[p]], axis=0)) for p in R]
    upd = [_dot(bk_t[p], jnp.concatenate([u_s[p], v_s[p]], axis=0)) for p in R]
    for p in R:
        y_ref[:, sls[p]] = o_s[p][:C] + o_s[p][C:]
        decay_col = jnp.broadcast_to(p_inc[p][C - 1:C, :], (LANES, LANES)).T
        st_scr[p] = (st[p] + upd[p]) * decay_col


def _rwkv_post_kernel(x_ref, y_ref, bonus_ref, g_ref, m_ref, gnw_ref, gnb_ref, gs_ref, gst_ref,
                      w_ref, o_ref):
    y = y_ref[...]
    inv_n = 1.0 / RWKV_HEAD
    mu = _head_sum(y, gs_ref, gst_ref) * inv_n
    d = y - mu
    var = _head_sum(d * d, gs_ref, gst_ref) * inv_n
    yn = (d * lax.rsqrt(var + RWKV_GN_EPS)) * gnw_ref[...] + gnb_ref[...]
    z = ((yn + bonus_ref[...]) * g_ref[...]).astype(BF16)
    o_ref[...] = x_ref[...] + m_ref[5:6, :] * _dot(z, w_ref[...])


def _rwkv7_mixer(x, mods, g, mu, w_rkv, w0, wd1, wd2, a0, wa1, wa2, wg1, wg2, k_k, k_a, r_k,
                 gn_w, gn_b, w_out, tm=256, pairs=8):
    B, S, D = x.shape
    H, N, C = RWKV_HEADS, RWKV_HEAD, RWKV_CHUNK
    tm = min(tm, S)
    h = _modnorm_call(x, mods, g)
    gs = (jnp.arange(D)[:, None] // N == jnp.arange(LANES)[None, :]).astype(BF16)
    H = LANES

    def pad_rank(w1, w2):
        r = w1.shape[1]
        rp = -(-r // LANES) * LANES
        return (jnp.pad(w1, ((0, 0), (0, rp - r))).astype(BF16),
                jnp.pad(w2, ((0, rp - r), (0, 0))).astype(BF16))

    wd1, wd2 = pad_rank(wd1, wd2)
    wa1, wa2 = pad_rank(wa1, wa2)
    wg1, wg2 = pad_rank(wg1, wg2)
    row = lambda t: t.reshape(1, D)
    full = lambda shape: _resident(shape, lambda b, i: (0,) * len(shape))
    tile = pl.BlockSpec((None, tm, D), lambda b, i: (b, i, 0))
    outs = pl.pallas_call(
        _rwkv_prep_kernel,
        grid=(B, S // tm),
        in_specs=[
            tile,
            pl.BlockSpec((None, 8, D), lambda b, i: (b, jnp.maximum(i * (tm // 8) - 1, 0), 0)),
            full((6, D)), full((3, D, D)), full((1, D)), full(wd1.shape), full(wd2.shape),
            full((1, D)), full(wa1.shape), full(wa2.shape), full(wg1.shape), full(wg2.shape),
            full((1, D)), full((1, D)), full((1, D)), full((D, H)), full((H, D)),
        ],
        out_specs=[tile] * 8,
        out_shape=[jax.ShapeDtypeStruct((B, S, D), F32)] * 8,
        compiler_params=_cparams(("parallel", "parallel")),
        name="rwkv_prep",
    )(h, h, mu, w_rkv.astype(BF16), row(w0), wd1, wd2, row(a0), wa1, wa2, wg1, wg2,
      row(k_k), row(k_a), row(r_k), gs, gs.T)
    r, lw, k, v, kk, a, gg, bonus = outs
    width = pairs * LANES
    blk = pl.BlockSpec((None, C, width), lambda b, hg, c: (b, c, hg))
    y = pl.pallas_call(
        functools.partial(_rwkv_scan_kernel, pairs=pairs),
        grid=(B, D // width, S // C),
        in_specs=[blk] * 6,
        out_specs=blk,
        out_shape=jax.ShapeDtypeStruct((B, S, D), F32),
        scratch_shapes=[pltpu.VMEM((pairs, LANES, LANES), F32)],
        compiler_params=_cparams(("parallel", "parallel", "arbitrary")),
        name="rwkv_scan",
    )(r, lw, k, v, kk, a)
    tp = min(512, S)
    tile_p = pl.BlockSpec((None, tp, D), lambda b, i: (b, i, 0))
    return pl.pallas_call(
        _rwkv_post_kernel,
        grid=(B, S // tp),
        in_specs=[tile_p, tile_p, tile_p, tile_p,
                  pl.BlockSpec((None, 9, D), lambda b, i: (b, 0, 0)),
                  full((1, D)), full((1, D)), full((D, H)), full((H, D)), full((D, D))],
        out_specs=tile_p,
        out_shape=jax.ShapeDtypeStruct((B, S, D), F32),
        compiler_params=_cparams(("parallel", "parallel")),
        name="rwkv_post",
    )(x, y, bonus, gg, mods, row(gn_w), row(gn_b), gs, gs.T, w_out.astype(BF16))


def kernel(x, c, ada_w, ada_b, norm_g, ffn_w_in, ffn_w_out, final_g, rel_table, mla_w_in, mla_q_norm, mla_w_uq, mla_kv_norm, mla_w_ukv, mla_w_out, rwkv_mu, rwkv_w_rkv, rwkv_w0, rwkv_wd1, rwkv_wd2, rwkv_a0, rwkv_wa1, rwkv_wa2, rwkv_wg1, rwkv_wg2, rwkv_k_k, rwkv_k_a, rwkv_r_k, rwkv_gn_w, rwkv_gn_b, rwkv_w_out, moba_w_in, moba_w_out, ret_w_in, ret_gn_w, ret_gn_b, ret_w_out):
    depth = ada_w.shape[0]
    mods_all = _ada_mods(c, ada_w, ada_b)
    ffn_in = ffn_w_in.astype(BF16)
    ffn_out = ffn_w_out.astype(BF16)
    for i in range(depth):
        mods = mods_all[i]
        x = _ffn(x, mods, norm_g[i, 0], ffn_in[i, 0], ffn_out[i, 0], sub=0)
        kind, j = i % 4, i // 4
        g = norm_g[i, 1]
        if kind == 0:
            x = _mla_mixer(x, mods, g, mla_w_in[j], mla_q_norm[j], mla_w_uq[j], mla_kv_norm[j],
                           mla_w_ukv[j], mla_w_out[j])
        elif kind == 1:
            x = _rwkv7_mixer(x, mods, g, rwkv_mu[j], rwkv_w_rkv[j], rwkv_w0[j], rwkv_wd1[j],
                             rwkv_wd2[j], rwkv_a0[j], rwkv_wa1[j], rwkv_wa2[j], rwkv_wg1[j],
                             rwkv_wg2[j], rwkv_k_k[j], rwkv_k_a[j], rwkv_r_k[j].reshape(-1),
                             rwkv_gn_w[j], rwkv_gn_b[j], rwkv_w_out[j])
        elif kind == 2:
            x = _moba_mixer(x, mods, g, moba_w_in[j], rel_table, moba_w_out[j])
        else:
            x = _retnet_mixer(x, mods, g, ret_w_in[j], ret_gn_w[j], ret_gn_b[j], ret_w_out[j])
        x = _ffn(x, mods, norm_g[i, 2], ffn_in[i, 1], ffn_out[i, 1], sub=2)
    return _final_norm(x, final_g)
```

```python
import functools
import math

import jax
import jax.numpy as jnp
from jax import lax
from jax.experimental import pallas as pl
from jax.experimental.pallas import tpu as pltpu

F32 = jnp.float32
BF16 = jnp.bfloat16
HI = lax.Precision.HIGHEST

LANES = 128
VMEM_LIMIT_BYTES = 56 * 1024 * 1024

D_MODEL = 1024
DEPTH = 4
EPS = 1e-6
MACARON_W = 0.5
D_FF = 2816
ROPE_BASE = 10000.0
REL_BUCKETS = 32
REL_MAX_DIST = 128
MLA_HEADS = 16
MLA_Q_RANK = 384
MLA_KV_RANK = 256
MLA_NOPE = 64
MLA_ROPE = 32
MLA_V = 64
RWKV_HEAD = 64
RWKV_HEADS = D_MODEL // RWKV_HEAD
RWKV_GN_EPS = 64e-5
RWKV_CHUNK = 64
MOBA_HEADS = 16
MOBA_HEAD_DIM = 64
MOBA_BLOCK = 256
MOBA_TOPK = 3
RET_HEADS = 4
RET_DK = 256
RET_DV = 512
RET_CHUNK = 128
RET_GN_EPS = 1e-5
NEG_BIG = -1e30
LOG2E = math.log2(math.e)


def _cparams(sem, vmem=VMEM_LIMIT_BYTES):
    return pltpu.CompilerParams(dimension_semantics=sem, vmem_limit_bytes=vmem)


def _resident(block_shape, index_map):
    return pl.BlockSpec(block_shape, index_map, pipeline_mode=pl.Buffered(1))


def _dot(a, b):
    return jnp.dot(a, b, preferred_element_type=F32)


def _dot_hi(a, b):
    return jnp.dot(a, b, preferred_element_type=F32, precision=HI)


def _split(x):
    hi = x.astype(BF16)
    return hi, (x - hi.astype(F32)).astype(BF16)


def _dot3(a, b):
    return _dot(a[0], b[0]) + (_dot(a[0], b[1]) + _dot(a[1], b[0]))


def _dot3_nt(a, b):
    return _dot_nt(a[0], b[0]) + (_dot_nt(a[0], b[1]) + _dot_nt(a[1], b[0]))


def _dot_nt(a, b, precision=None):
    return lax.dot_general(a, b, (((1,), (1,)), ((), ())),
                           preferred_element_type=F32, precision=precision)


def _silu(x):
    return x * jax.nn.sigmoid(x)


def _modnorm(x, g, shift, scale):
    y = x * lax.rsqrt(jnp.mean(x * x, axis=-1, keepdims=True) + EPS)
    return (y * g) * (1 + scale) + shift


def _rms(x, g):
    return (x * lax.rsqrt(jnp.mean(x * x, axis=-1, keepdims=True) + EPS)) * g


def _ada_kernel(c_ref, w_ref, b_ref, o_ref):
    o_ref[...] = _dot_hi(_silu(c_ref[...]), w_ref[...]) + b_ref[...]


def _ada_mods(c, ada_w, ada_b):
    B, D = c.shape
    depth, _, N = ada_w.shape
    rows = 8
    cp = jnp.zeros((rows, D), F32).at[:B].set(c)
    tn = N // 8
    out = pl.pallas_call(
        _ada_kernel,
        grid=(depth, N // tn),
        in_specs=[
            pl.BlockSpec((rows, D), lambda l, n: (0, 0)),
            pl.BlockSpec((None, D, tn), lambda l, n: (l, 0, n)),
            pl.BlockSpec((None, 1, tn), lambda l, n: (l, 0, n)),
        ],
        out_specs=pl.BlockSpec((None, rows, tn), lambda l, n: (l, 0, n)),
        out_shape=jax.ShapeDtypeStruct((depth, rows, N), F32),
        compiler_params=_cparams(("parallel", "parallel")),
        name="ada_mods",
    )(cp, ada_w, ada_b.reshape(depth, 1, N))
    return out[:, :B].reshape(depth, B, 9, D)


def _ffn_kernel(x_ref, m_ref, g_ref, wg_ref, wu_ref, wo_ref, o_ref, *, sub, tf):
    x = x_ref[...]
    shift = m_ref[3 * sub:3 * sub + 1, :]
    scale = m_ref[3 * sub + 1:3 * sub + 2, :]
    gate = m_ref[3 * sub + 2:3 * sub + 3, :]
    h = _modnorm(x, g_ref[...], shift, scale).astype(BF16)
    acc = jnp.zeros(x.shape, F32)
    for j in range(wg_ref.shape[1] // tf):
        gt = _dot(h, wg_ref[:, j * tf:(j + 1) * tf])
        up = _dot(h, wu_ref[:, j * tf:(j + 1) * tf])
        a = (_silu(gt) * up).astype(BF16)
        acc = acc + _dot(a, wo_ref[j * tf:(j + 1) * tf, :])
    o_ref[...] = x + (MACARON_W * gate) * acc


def _ffn(x, mods, g, w_in, w_out, layer, which, sub, tm=512):
    B, S, D = x.shape
    dff = w_out.shape[2]
    tm = min(tm, S)
    tf = dff // 2
    return pl.pallas_call(
        functools.partial(_ffn_kernel, sub=sub, tf=tf),
        grid=(B, S // tm),
        in_specs=[
            pl.BlockSpec((None, tm, D), lambda b, i: (b, i, 0)),
            pl.BlockSpec((None, 9, D), lambda b, i: (b, 0, 0)),
            pl.BlockSpec((1, D), lambda b, i: (0, 0)),
            _resident((None, None, D, dff), lambda b, i: (layer, which, 0, 0)),
            _resident((None, None, D, dff), lambda b, i: (layer, which, 0, 1)),
            _resident((None, None, dff, D), lambda b, i: (layer, which, 0, 0)),
        ],
        out_specs=pl.BlockSpec((None, tm, D), lambda b, i: (b, i, 0)),
        out_shape=jax.ShapeDtypeStruct((B, S, D), F32),
        compiler_params=_cparams(("parallel", "parallel")),
        name="ffn",
    )(x, mods, g.reshape(1, D), w_in, w_in, w_out)


def _proj_kernel(x_ref, m_ref, g_ref, w_ref, *o_and_scr, sub):
    o_refs, h_scr = o_and_scr[:-1], o_and_scr[-1]

    @pl.when(pl.program_id(2) == 0)
    def _():
        h_scr[...] = _modnorm(x_ref[...], g_ref[...], m_ref[3 * sub:3 * sub + 1, :],
                              m_ref[3 * sub + 1:3 * sub + 2, :]).astype(BF16)

    y = _dot(h_scr[...], w_ref[...])
    for o_ref in o_refs:
        o_ref[...] = y.astype(o_ref.dtype)


def _proj(x, mods, g, w, out_dtypes, sub=1, tm=1024, tn=1024):
    B, S, D = x.shape
    N = w.shape[1]
    tm = min(tm, S)
    tn = tn if N % tn == 0 else N
    outs = pl.pallas_call(
        functools.partial(_proj_kernel, sub=sub),
        grid=(B, S // tm, N // tn),
        in_specs=[
            pl.BlockSpec((None, tm, D), lambda b, i, n: (b, i, 0)),
            pl.BlockSpec((None, 9, D), lambda b, i, n: (b, 0, 0)),
            pl.BlockSpec((1, D), lambda b, i, n: (0, 0)),
            pl.BlockSpec((D, tn), lambda b, i, n: (0, n)),
        ],
        out_specs=[pl.BlockSpec((None, tm, tn), lambda b, i, n: (b, i, n)) for _ in out_dtypes],
        out_shape=[jax.ShapeDtypeStruct((B, S, N), dt) for dt in out_dtypes],
        scratch_shapes=[pltpu.VMEM((tm, D), BF16)],
        compiler_params=_cparams(("parallel", "parallel", "arbitrary")),
        name="modnorm_proj",
    )(x, mods, g.reshape(1, D), w)
    return outs


def _proj_t_kernel(x_ref, m_ref, g_ref, wt_ref, o_ref, *, sub):
    h = _modnorm(x_ref[...], g_ref[...], m_ref[3 * sub:3 * sub + 1, :],
                 m_ref[3 * sub + 1:3 * sub + 2, :]).astype(BF16)
    o_ref[...] = _dot_nt(wt_ref[...], h).astype(o_ref.dtype)


def _proj_t(x, mods, g, wt, out_dtype, sub=1, tm=1024):
    B, S, D = x.shape
    N = wt.shape[0]
    tm = min(tm, S)
    return pl.pallas_call(
        functools.partial(_proj_t_kernel, sub=sub),
        grid=(B, S // tm),
        in_specs=[
            pl.BlockSpec((None, tm, D), lambda b, i: (b, i, 0)),
            pl.BlockSpec((None, 9, D), lambda b, i: (b, 0, 0)),
            pl.BlockSpec((1, D), lambda b, i: (0, 0)),
            _resident((N, D), lambda b, i: (0, 0)),
        ],
        out_specs=pl.BlockSpec((None, N, tm), lambda b, i: (b, 0, i)),
        out_shape=jax.ShapeDtypeStruct((B, N, S), out_dtype),
        compiler_params=_cparams(("parallel", "parallel")),
        name="modnorm_proj_t",
    )(x, mods, g.reshape(1, D), wt)


def _norm_kernel(x_ref, m_ref, g_ref, o_ref, *, sub):
    o_ref[...] = _modnorm(x_ref[...], g_ref[...], m_ref[3 * sub:3 * sub + 1, :],
                          m_ref[3 * sub + 1:3 * sub + 2, :])


def _modnorm_call(x, mods, g, sub=1, tm=1024):
    B, S, D = x.shape
    tm = min(tm, S)
    return pl.pallas_call(
        functools.partial(_norm_kernel, sub=sub),
        grid=(B, S // tm),
        in_specs=[
            pl.BlockSpec((None, tm, D), lambda b, i: (b, i, 0)),
            pl.BlockSpec((None, 9, D), lambda b, i: (b, 0, 0)),
            pl.BlockSpec((1, D), lambda b, i: (0, 0)),
        ],
        out_specs=pl.BlockSpec((None, tm, D), lambda b, i: (b, i, 0)),
        out_shape=jax.ShapeDtypeStruct((B, S, D), F32),
        compiler_params=_cparams(("parallel", "parallel")),
        name="modnorm",
    )(x, mods, g.reshape(1, D))


def _out_kernel(x_ref, y_ref, m_ref, w_ref, o_ref):
    o_ref[...] = x_ref[...] + m_ref[5:6, :] * _dot(y_ref[...], w_ref[...])


def _out_proj(x, y, mods, w, tm=512):
    B, S, D = x.shape
    K = w.shape[0]
    tm = min(tm, S)
    return pl.pallas_call(
        _out_kernel,
        grid=(B, S // tm),
        in_specs=[
            pl.BlockSpec((None, tm, D), lambda b, i: (b, i, 0)),
            pl.BlockSpec((None, tm, K), lambda b, i: (b, i, 0)),
            pl.BlockSpec((None, 9, D), lambda b, i: (b, 0, 0)),
            _resident((K, D), lambda b, i: (0, 0)),
        ],
        out_specs=pl.BlockSpec((None, tm, D), lambda b, i: (b, i, 0)),
        out_shape=jax.ShapeDtypeStruct((B, S, D), F32),
        compiler_params=_cparams(("parallel", "parallel")),
        name="out_proj",
    )(x, y, mods, w)


def _final_kernel(x_ref, g_ref, o_ref):
    o_ref[...] = _rms(x_ref[...], g_ref[...])


def _final_norm(x, g, tm=1024):
    B, S, D = x.shape
    tm = min(tm, S)
    return pl.pallas_call(
        _final_kernel,
        grid=(B, S // tm),
        in_specs=[pl.BlockSpec((None, tm, D), lambda b, i: (b, i, 0)),
                  pl.BlockSpec((1, D), lambda b, i: (0, 0))],
        out_specs=pl.BlockSpec((None, tm, D), lambda b, i: (b, i, 0)),
        out_shape=jax.ShapeDtypeStruct((B, S, D), F32),
        compiler_params=_cparams(("parallel", "parallel")),
        name="final_norm",
    )(x, g.reshape(1, D))


def _rope_tables(seq, dim):
    inv = ROPE_BASE ** (-jnp.arange(0, dim, 2, dtype=F32) / dim)
    ang = jnp.arange(seq, dtype=F32)[:, None] * inv[None, :]
    return jnp.cos(ang), jnp.sin(ang)


def _mla_prep_kernel(p_ref, qn_ref, kvn_ref, wq_ref, wqs_ref, wk_ref, wvt_ref, c_ref, s_ref,
                     q_ref, k_ref, vt_ref):
    p = p_ref[...]
    cq = _rms(p[:, :MLA_Q_RANK], qn_ref[...]).astype(BF16)
    ckv = _rms(p[:, MLA_Q_RANK:MLA_Q_RANK + MLA_KV_RANK], kvn_ref[...]).astype(BF16)
    o = MLA_Q_RANK + MLA_KV_RANK
    cos = c_ref[...]
    sin = s_ref[...]
    k_rope = p[:, o:o + LANES] * cos + p[:, o + LANES:o + 2 * LANES] * sin
    q1 = _dot(cq, wq_ref[...])
    q2 = _dot(cq, wqs_ref[...])
    kn = _dot(ckv, wk_ref[...])
    scale = (MLA_NOPE + MLA_ROPE) ** -0.5 * LOG2E
    for h in range(MLA_HEADS):
        sl = slice(h * LANES, (h + 1) * LANES)
        q_ref[:, sl] = ((q1[:, sl] * cos + q2[:, sl] * sin) * scale).astype(BF16)
        k_ref[:, sl] = (kn[:, sl] + k_rope).astype(BF16)
    vt_ref[...] = _dot_nt(wvt_ref[...], ckv).astype(BF16)


def _flash_update(s_t, vt, m_scr, l_scr, acc_scr, h):
    m_old = m_scr[h]
    m_new = jnp.maximum(m_old, jnp.max(s_t, axis=0, keepdims=True))
    alpha = jnp.exp2(m_old - m_new)
    p_t = jnp.exp2(s_t - m_new)
    l_scr[h] = alpha * l_scr[h] + jnp.sum(p_t, axis=0, keepdims=True)
    acc_scr[h] = alpha * acc_scr[h] + _dot(vt, p_t.astype(BF16))
    m_scr[h] = m_new


def _flash_init(m_scr, l_scr, acc_scr):
    m_scr[...] = jnp.full(m_scr.shape, -jnp.inf, F32)
    l_scr[...] = jnp.zeros(l_scr.shape, F32)
    acc_scr[...] = jnp.zeros(acc_scr.shape, F32)


def _flash_finish(o_ref, l_scr, acc_scr):
    o_t = jnp.concatenate([acc_scr[h] / l_scr[h] for h in range(acc_scr.shape[0])], axis=0)
    o_ref[...] = o_t.T.astype(o_ref.dtype)


def _mla_attn_kernel(q_ref, k_ref, vt_ref, o_ref, s_scr, m_scr, l_scr, acc_scr, *, t):
    i = pl.program_id(2)
    nh = acc_scr.shape[0]
    _flash_init(m_scr, l_scr, acc_scr)
    q = q_ref[...]

    def scores(j, slot):
        kb = k_ref[pl.ds(pl.multiple_of(j * t, t), t), :]
        for h in range(nh):
            s_scr[slot, h] = _dot_nt(kb[:, h * LANES:(h + 1) * LANES], q[:, h * LANES:(h + 1) * LANES])

    def consume(j, slot, diagonal):
        vtb = vt_ref[:, pl.ds(pl.multiple_of(j * t, t), t)]
        for h in range(nh):
            s_t = s_scr[slot, h]
            if diagonal:
                key = lax.broadcasted_iota(jnp.int32, s_t.shape, 0)
                qry = lax.broadcasted_iota(jnp.int32, s_t.shape, 1)
                s_t = jnp.where(key <= qry, s_t, -jnp.inf)
            _flash_update(s_t, vtb[h * MLA_V:(h + 1) * MLA_V, :], m_scr, l_scr, acc_scr, h)

    scores(0, 0)

    def body(p, carry):
        j = 2 * p
        scores(j + 1, 1)
        consume(j, 0, False)
        scores(j + 2, 0)
        consume(j + 1, 1, False)
        return carry

    lax.fori_loop(0, i // 2, body, 0)

    @pl.when(i % 2 == 0)
    def _():
        consume(i, 0, True)

    @pl.when(i % 2 == 1)
    def _():
        scores(i, 1)
        consume(i - 1, 0, False)
        consume(i, 1, True)

    _flash_finish(o_ref, l_scr, acc_scr)


def _mla_weights(w_in, w_uq, w_ukv):
    H, dn, dr, dv = MLA_HEADS, MLA_NOPE, MLA_ROPE, MLA_V
    D = w_in.shape[0]
    o = MLA_Q_RANK + MLA_KV_RANK
    kr = w_in[:, o:]
    z64 = jnp.zeros((D, dn), F32)
    z32 = jnp.zeros((D, LANES - dn - dr), F32)
    half = dr // 2
    w_in_p = jnp.concatenate(
        [w_in[:, :o], z64, kr, z32, z64, -kr[:, half:], kr[:, :half], z32], axis=1)
    wq = w_uq.reshape(MLA_Q_RANK, H, dn + dr)
    zq = jnp.zeros((MLA_Q_RANK, H, LANES - dn - dr), F32)
    wq_p = jnp.concatenate([wq, zq], axis=2).reshape(MLA_Q_RANK, H * LANES)
    wqs_p = jnp.concatenate(
        [jnp.zeros((MLA_Q_RANK, H, dn), F32), -wq[:, :, dn + half:], wq[:, :, dn:dn + half], zq],
        axis=2).reshape(MLA_Q_RANK, H * LANES)
    wkv = w_ukv.reshape(MLA_KV_RANK, H, dn + dv)
    wk_p = jnp.concatenate(
        [wkv[:, :, :dn], jnp.zeros((MLA_KV_RANK, H, LANES - dn), F32)], axis=2).reshape(MLA_KV_RANK, H * LANES)
    wvt_p = wkv[:, :, dn:].reshape(MLA_KV_RANK, H * dv).T
    return (w_in_p.astype(BF16), wq_p.astype(BF16), wqs_p.astype(BF16),
            wk_p.astype(BF16), wvt_p.astype(BF16))


def _mla_mixer(x, mods, g, w_in, q_norm, w_uq, kv_norm, w_ukv, w_out, tm=512, t=256, nh=8):
    B, S, D = x.shape
    H = MLA_HEADS
    tm = min(tm, S)
    t = min(t, S)
    w_in_p, wq_p, wqs_p, wk_p, wvt_p = _mla_weights(w_in, w_uq, w_ukv)
    (p,) = _proj(x, mods, g, w_in_p, [F32])
    cos, sin = _rope_tables(S, MLA_ROPE)
    one = jnp.ones((S, MLA_NOPE), F32)
    pad = LANES - MLA_NOPE - MLA_ROPE
    tab_c = jnp.concatenate([one, cos, cos, jnp.ones((S, pad), F32)], axis=1)
    tab_s = jnp.concatenate([0 * one, sin, sin, jnp.zeros((S, pad), F32)], axis=1)
    NP = p.shape[-1]
    q, k, vt = pl.pallas_call(
        _mla_prep_kernel,
        grid=(B, S // tm),
        in_specs=[
            pl.BlockSpec((None, tm, NP), lambda b, i: (b, i, 0)),
            pl.BlockSpec((1, MLA_Q_RANK), lambda b, i: (0, 0)),
            pl.BlockSpec((1, MLA_KV_RANK), lambda b, i: (0, 0)),
            _resident(wq_p.shape, lambda b, i: (0, 0)),
            _resident(wqs_p.shape, lambda b, i: (0, 0)),
            _resident(wk_p.shape, lambda b, i: (0, 0)),
            _resident(wvt_p.shape, lambda b, i: (0, 0)),
            pl.BlockSpec((tm, LANES), lambda b, i: (i, 0)),
            pl.BlockSpec((tm, LANES), lambda b, i: (i, 0)),
        ],
        out_specs=[
            pl.BlockSpec((None, tm, H * LANES), lambda b, i: (b, i, 0)),
            pl.BlockSpec((None, tm, H * LANES), lambda b, i: (b, i, 0)),
            pl.BlockSpec((None, H * MLA_V, tm), lambda b, i: (b, 0, i)),
        ],
        out_shape=[
            jax.ShapeDtypeStruct((B, S, H * LANES), BF16),
            jax.ShapeDtypeStruct((B, S, H * LANES), BF16),
            jax.ShapeDtypeStruct((B, H * MLA_V, S), BF16),
        ],
        compiler_params=_cparams(("parallel", "parallel")),
        name="mla_prep",
    )(p, q_norm.reshape(1, -1), kv_norm.reshape(1, -1), wq_p, wqs_p, wk_p, wvt_p, tab_c, tab_s)
    o = pl.pallas_call(
        functools.partial(_mla_attn_kernel, t=t),
        grid=(B, H // nh, S // t),
        in_specs=[
            pl.BlockSpec((None, t, nh * LANES), lambda b, hg, i: (b, i, hg)),
            _resident((None, S, nh * LANES), lambda b, hg, i: (b, 0, hg)),
            _resident((None, nh * MLA_V, S), lambda b, hg, i: (b, hg, 0)),
        ],
        out_specs=pl.BlockSpec((None, t, nh * MLA_V), lambda b, hg, i: (b, i, hg)),
        out_shape=jax.ShapeDtypeStruct((B, S, H * MLA_V), BF16),
        scratch_shapes=[pltpu.VMEM((2, nh, t, t), F32),
                        pltpu.VMEM((nh, 1, t), F32), pltpu.VMEM((nh, 1, t), F32),
                        pltpu.VMEM((nh, MLA_V, t), F32)],
        compiler_params=_cparams(("parallel", "parallel", "arbitrary")),
        name="mla_attn",
    )(q, k, vt)
    return _out_proj(x, o, mods, w_out.astype(BF16))


def _moba_kmean_kernel(k_ref, o_ref):
    o_ref[...] = jnp.mean(k_ref[...], axis=0, keepdims=True)


def _moba_select_kernel(q_ref, km_ref, o_ref, *, nb):
    H, dh, tq = MOBA_HEADS, MOBA_HEAD_DIM, MOBA_BLOCK
    own = pl.program_id(1)
    assert nb & (nb - 1) == 0 and dh & (dh - 1) == 0
    hs = 2 * LANES // dh
    shape = (hs * nb, 2 * LANES)
    row_head = lax.shift_right_logical(lax.broadcasted_iota(jnp.int32, shape, 0), nb.bit_length() - 1)
    lane_head = lax.shift_right_logical(lax.broadcasted_iota(jnp.int32, shape, 1), dh.bit_length() - 1)
    gates = []
    for c in range(H // hs):
        sl = slice(c * 2 * LANES, (c + 1) * 2 * LANES)
        kbd = jnp.concatenate([km_ref[:, sl]] * hs, axis=0)
        kbd = jnp.where(row_head == lane_head, kbd, 0.0)
        gates.append(_dot3_nt(_split(kbd), _split(q_ref[:, sl])))
    gate = jnp.concatenate(gates, axis=0)
    g = gate.reshape(H, nb, tq)
    n_iota = lax.broadcasted_iota(jnp.int32, g.shape, 1)
    g = jnp.where(n_iota < own, g, -jnp.inf)
    sel = jnp.zeros(g.shape, jnp.bool_)
    for _ in range(min(MOBA_TOPK, nb)):
        m = jnp.max(g, axis=1, keepdims=True)
        first = jnp.min(jnp.where(g == m, n_iota, nb), axis=1, keepdims=True)
        pick = n_iota == first
        sel = sel | (pick & (m > -jnp.inf))
        g = jnp.where(pick, -jnp.inf, g)
    bias = jnp.where(sel | (n_iota == own), 0.0, NEG_BIG)
    if nb < LANES:
        bias = jnp.concatenate([bias, jnp.zeros((H, LANES - nb, tq), F32)], axis=1)
    o_ref[...] = bias.reshape(H * LANES, tq).T.astype(o_ref.dtype)


def _moba_bias_kernel(tbl_ref, o_ref):
    h = pl.program_id(0)
    blk = MOBA_BLOCK
    key = lax.broadcasted_iota(jnp.int32, (blk, blk), 0)
    qry = lax.broadcasted_iota(jnp.int32, (blk, blk), 1)
    max_exact = REL_BUCKETS // 2
    for t in range(2):
        n = jnp.maximum(qry - key + blk * t, 0)
        nf = jnp.maximum(n, max_exact).astype(F32)
        large = max_exact + (jnp.log(nf / max_exact) / math.log(REL_MAX_DIST / max_exact)
                             * (REL_BUCKETS - max_exact)).astype(jnp.int32)
        large = jnp.minimum(large, REL_BUCKETS - 1)
        bucket = jnp.where(n < max_exact, n, large)
        acc = jnp.zeros((blk, blk), F32)
        for b in range(REL_BUCKETS):
            acc = jnp.where(bucket == b, tbl_ref[b, h], acc)
        o_ref[t] = acc * LOG2E


def _moba_attn_kernel(tbl_ref, q_ref, k_ref, vt_ref, sel_ref, bias_ref, o_ref,
                      s_scr, m_scr, l_scr, acc_scr):
    hg = pl.program_id(1)
    i = pl.program_id(2)
    t = MOBA_BLOCK
    dh = MOBA_HEAD_DIM
    nh = acc_scr.shape[0]
    _flash_init(m_scr, l_scr, acc_scr)
    lane = lax.broadcasted_iota(jnp.int32, (t, LANES), 1)
    qa = []
    for h in range(nh):
        q = q_ref[:, (h // 2) * LANES:(h // 2 + 1) * LANES]
        q = jnp.where((lane < dh) == (h % 2 == 0), q, 0)
        qa.append(jnp.concatenate([q, sel_ref[:, h * LANES:(h + 1) * LANES]], axis=1))

    def scores(j, slot):
        kb = k_ref[pl.ds(pl.multiple_of(j * t, t), t), :]
        onehot = jnp.where(lane == j, 1.0, 0.0).astype(BF16)
        for h in range(nh):
            ka = jnp.concatenate([kb[:, (h // 2) * LANES:(h // 2 + 1) * LANES], onehot], axis=1)
            s_scr[slot, h] = _dot_nt(ka, qa[h])

    FAR, PREV, OWN = 0, 1, 2

    def consume(j, slot, kind):
        vtb = vt_ref[:, pl.ds(pl.multiple_of(j * t, t), t)]
        for h in range(nh):
            if kind == FAR:
                s_t = s_scr[slot, h] + tbl_ref[REL_BUCKETS - 1, nh * hg + h] * LOG2E
            elif kind == PREV:
                s_t = s_scr[slot, h] + bias_ref[h, 1]
            else:
                key = lax.broadcasted_iota(jnp.int32, (t, t), 0)
                qry = lax.broadcasted_iota(jnp.int32, (t, t), 1)
                s_t = jnp.where(key <= qry, s_scr[slot, h] + bias_ref[h, 0], -jnp.inf)
            _flash_update(s_t, vtb[h * dh:(h + 1) * dh, :], m_scr, l_scr, acc_scr, h)

    scores(0, 0)
    n_far = jnp.maximum(i - 1, 0)

    def body(p, carry):
        j = 2 * p
        scores(j + 1, 1)
        consume(j, 0, FAR)
        scores(j + 2, 0)
        consume(j + 1, 1, FAR)
        return carry

    lax.fori_loop(0, n_far // 2, body, 0)

    @pl.when(i == 0)
    def _():
        consume(i, 0, OWN)

    @pl.when(i % 2 == 1)
    def _():
        scores(i, 1)
        consume(i - 1, 0, PREV)
        consume(i, 1, OWN)

    @pl.when((i % 2 == 0) & (i > 0))
    def _():
        scores(i - 1, 1)
        consume(i - 2, 0, FAR)
        scores(i, 0)
        consume(i - 1, 1, PREV)
        consume(i, 0, OWN)

    _flash_finish(o_ref, l_scr, acc_scr)


def _moba_mixer(x, mods, g, w_in, rel_table, w_out, nh=8):
    B, S, D = x.shape
    H, t = MOBA_HEADS, MOBA_BLOCK
    nb = S // t
    col_scale = jnp.where(jnp.arange(2 * D) < D, MOBA_HEAD_DIM ** -0.5 * LOG2E, 1.0).astype(F32)
    qkv, qkv16 = _proj(x, mods, g, (w_in[:, :2 * D] * col_scale).astype(BF16), [F32, BF16])
    vt = _proj_t(x, mods, g, w_in[:, 2 * D:].T.astype(BF16), BF16)
    kmean = pl.pallas_call(
        _moba_kmean_kernel,
        grid=(B, nb),
        in_specs=[pl.BlockSpec((None, t, D), lambda b, n: (b, n, 1))],
        out_specs=pl.BlockSpec((None, None, 1, D), lambda b, n: (b, n, 0, 0)),
        out_shape=jax.ShapeDtypeStruct((B, nb, 1, D), F32),
        compiler_params=_cparams(("parallel", "parallel")),
        name="moba_kmean",
    )(qkv).reshape(B, nb, D)
    sel = pl.pallas_call(
        functools.partial(_moba_select_kernel, nb=nb),
        grid=(B, nb),
        in_specs=[pl.BlockSpec((None, t, D), lambda b, i: (b, i, 0)),
                  pl.BlockSpec((None, nb, D), lambda b, i: (b, 0, 0))],
        out_specs=pl.BlockSpec((None, t, H * LANES), lambda b, i: (b, i, 0)),
        out_shape=jax.ShapeDtypeStruct((B, S, H * LANES), BF16),
        compiler_params=_cparams(("parallel", "parallel")),
        name="moba_select",
    )(qkv, kmean)
    bias = pl.pallas_call(
        _moba_bias_kernel,
        grid=(H,),
        in_specs=[pl.BlockSpec(memory_space=pltpu.SMEM)],
        out_specs=pl.BlockSpec((None, 2, t, t), lambda h: (h, 0, 0, 0)),
        out_shape=jax.ShapeDtypeStruct((H, 2, t, t), F32),
        compiler_params=_cparams(("parallel",)),
        name="moba_bias",
    )(rel_table)
    ng = H // nh
    w = nh * MOBA_HEAD_DIM
    o = pl.pallas_call(
        _moba_attn_kernel,
        grid=(B, ng, nb),
        in_specs=[
            pl.BlockSpec(memory_space=pltpu.SMEM),
            pl.BlockSpec((None, t, w), lambda b, hg, i: (b, i, hg)),
            _resident((None, S, w), lambda b, hg, i: (b, 0, ng + hg)),
            _resident((None, w, S), lambda b, hg, i: (b, hg, 0)),
            pl.BlockSpec((None, t, nh * LANES), lambda b, hg, i: (b, i, hg)),
            pl.BlockSpec((nh, 2, t, t), lambda b, hg, i: (hg, 0, 0, 0)),
        ],
        out_specs=pl.BlockSpec((None, t, w), lambda b, hg, i: (b, i, hg)),
        out_shape=jax.ShapeDtypeStruct((B, S, D), BF16),
        scratch_shapes=[pltpu.VMEM((2, nh, t, t), F32),
                        pltpu.VMEM((nh, 1, t), F32), pltpu.VMEM((nh, 1, t), F32),
                        pltpu.VMEM((nh, MOBA_HEAD_DIM, t), F32)],
        compiler_params=_cparams(("parallel", "parallel", "arbitrary")),
        name="moba_attn",
    )(rel_table, qkv16, qkv16, vt, sel, bias)
    return _out_proj(x, o, mods, w_out.astype(BF16))


def _ret_kernel(gam_ref, q_ref, k_ref, v_ref, g_ref, cos_ref, sin_ref, dm_ref, zeta_ref, xi_ref,
                gnw_ref, gnb_ref, o_ref, r_scr):
    @pl.when(pl.program_id(1) == 0)
    def _():
        r_scr[...] = jnp.zeros(r_scr.shape, F32)

    cos = cos_ref[...]
    sin = sin_ref[...]
    DK, DV = RET_DK, RET_DV
    half = DK // 2
    heads = range(RET_HEADS)

    def rope(t):
        t1, t2 = t[:, :half], t[:, half:]
        return jnp.concatenate([t1 * cos - t2 * sin, t1 * sin + t2 * cos], axis=1)

    q16 = [rope(q_ref[:, h * DK:(h + 1) * DK]).astype(BF16) for h in heads]
    k = [rope(k_ref[:, h * DK:(h + 1) * DK] * DK ** -0.5) for h in heads]
    v16 = [v_ref[:, h * DV:(h + 1) * DV].astype(BF16) for h in heads]
    R = [r_scr[h] for h in heads]
    inner = [(_dot_nt(q16[h], k[h].astype(BF16)) * dm_ref[h]).astype(BF16) for h in heads]
    cross = [_dot(q16[h], R[h].astype(BF16)) for h in heads]
    o = [_dot(inner[h], v16[h]) + cross[h] * xi_ref[h] for h in heads]
    kv = [_dot((k[h] * zeta_ref[h]).T.astype(BF16), v16[h]) for h in heads]
    for h in heads:
        r_scr[h] = gam_ref[h] * R[h] + kv[h]
        mu = jnp.mean(o[h], axis=-1, keepdims=True)
        d = o[h] - mu
        var = jnp.mean(d * d, axis=-1, keepdims=True)
        sl = slice(h * DV, (h + 1) * DV)
        yn = (d * lax.rsqrt(var + RET_GN_EPS)) * gnw_ref[:, sl] + gnb_ref[:, sl]
        o_ref[:, sl] = (_silu(g_ref[:, sl]) * yn).astype(o_ref.dtype)


def _retnet_mixer(x, mods, g, w_in, gn_w, gn_b, w_out):
    B, S, D = x.shape
    H, DK, DV, C = RET_HEADS, RET_DK, RET_DV, RET_CHUNK
    C = min(C, S)
    (p,) = _proj(x, mods, g, w_in.astype(BF16), [F32])
    cos, sin = _rope_tables(S, DK)
    log_gamma = jnp.log1p(-jnp.exp2(-5.0 - jnp.arange(H, dtype=F32)))
    idx = jnp.arange(C, dtype=F32)
    diff = idx[:, None] - idx[None, :]
    dmask = jnp.where(diff >= 0, jnp.exp(jnp.maximum(diff, 0.0)[None] * log_gamma[:, None, None]), 0.0)
    zeta = jnp.exp((C - 1 - idx)[None, :] * log_gamma[:, None])[:, :, None]
    xi = jnp.exp((idx + 1)[None, :] * log_gamma[:, None])[:, :, None]
    gamma_c = jnp.exp(C * log_gamma)
    y = pl.pallas_call(
        _ret_kernel,
        grid=(B, S // C),
        in_specs=[
            pl.BlockSpec(memory_space=pltpu.SMEM),
            pl.BlockSpec((None, C, D), lambda b, c: (b, c, 0)),
            pl.BlockSpec((None, C, D), lambda b, c: (b, c, 1)),
            pl.BlockSpec((None, C, 2 * D), lambda b, c: (b, c, 1)),
            pl.BlockSpec((None, C, 2 * D), lambda b, c: (b, c, 2)),
            pl.BlockSpec((C, DK // 2), lambda b, c: (c, 0)),
            pl.BlockSpec((C, DK // 2), lambda b, c: (c, 0)),
            pl.BlockSpec((H, C, C), lambda b, c: (0, 0, 0)),
            pl.BlockSpec((H, C, 1), lambda b, c: (0, 0, 0)),
            pl.BlockSpec((H, C, 1), lambda b, c: (0, 0, 0)),
            pl.BlockSpec((1, H * DV), lambda b, c: (0, 0)),
            pl.BlockSpec((1, H * DV), lambda b, c: (0, 0)),
        ],
        out_specs=pl.BlockSpec((None, C, H * DV), lambda b, c: (b, c, 0)),
        out_shape=jax.ShapeDtypeStruct((B, S, H * DV), BF16),
        scratch_shapes=[pltpu.VMEM((H, DK, DV), F32)],
        compiler_params=_cparams(("parallel", "arbitrary")),
        name="retention",
    )(gamma_c, p, p, p, p, cos, sin, dmask, zeta, xi, gn_w.reshape(1, -1), gn_b.reshape(1, -1))
    return _out_proj(x, y, mods, w_out.astype(BF16))


def _head_sum(t, gs_ref, gst_ref):
    def times_indicator(x, m):
        hi, lo = _split(x)
        return _dot(hi, m) + _dot(lo, m)

    return times_indicator(times_indicator(t, gs_ref[...]), gst_ref[...])


def _rwkv_prep_kernel(h_ref, hp_ref, mu_ref, wrkv_ref, w0_ref, wd1_ref, wd2_ref, a0_ref, wa1_ref,
                      wa2_ref, wg1_ref, wg2_ref, kk_ref, ka_ref, rk_ref, gs_ref, gst_ref,
                      r_o, lw_o, k_o, v_o, kk_o, a_o, g_o, bonus_o):
    i = pl.program_id(1)
    h = h_ref[...]
    tm = h.shape[0]
    prev_tail = jnp.where(i > 0, hp_ref[7:8, :], 0.0)
    rolled = pltpu.roll(h, 1, axis=0)
    row = lax.broadcasted_iota(jnp.int32, h.shape, 0)
    xx = jnp.where(row == 0, prev_tail, rolled) - h
    xs = [h + xx * mu_ref[n:n + 1, :] for n in range(6)]
    r = _dot(xs[0].astype(BF16), wrkv_ref[0])
    k = _dot(xs[1].astype(BF16), wrkv_ref[1])
    v = _dot(xs[2].astype(BF16), wrkv_ref[2])
    def low_rank(t, w1_ref, w2_ref, act):
        return _dot(act(_dot(t.astype(BF16), w1_ref[...])).astype(BF16), w2_ref[...])

    z = -(w0_ref[...] + low_rank(xs[3], wd1_ref, wd2_ref, jnp.tanh))
    softplus = jnp.maximum(z, 0.0) + jnp.log(1.0 + jnp.exp(-jnp.abs(z)))
    w = -softplus - 0.5
    lw_o[...] = -jnp.exp(w)
    a = jax.nn.sigmoid(a0_ref[...] + low_rank(xs[4], wa1_ref, wa2_ref, lambda t: t))
    g_o[...] = low_rank(xs[5], wg1_ref, wg2_ref, jax.nn.sigmoid).astype(g_o.dtype)
    kk = k * kk_ref[...]
    norm = jnp.sqrt(_head_sum(kk * kk, gs_ref, gst_ref))
    kk_o[...] = (kk / jnp.maximum(norm, 1e-12)).astype(kk_o.dtype)
    k_mod = k * (1 + (a - 1) * ka_ref[...])
    r_o[...] = r.astype(r_o.dtype)
    k_o[...] = k_mod.astype(k_o.dtype)
    v_o[...] = v.astype(v_o.dtype)
    a_o[...] = a.astype(a_o.dtype)
    bonus_o[...] = (_head_sum(r * k_mod * rk_ref[...], gs_ref, gst_ref) * v).astype(bonus_o.dtype)


def _rwkv_scan_kernel(r_ref, lw_ref, k_ref, v_ref, kk_ref, a_ref, y_ref, st_scr, *, pairs):
    C = RWKV_CHUNK
    P = 2 * C

    @pl.when(pl.program_id(2) == 0)
    def _():
        st_scr[...] = jnp.zeros(st_scr.shape, F32)

    tri = (lax.broadcasted_iota(jnp.int32, (C, C), 0) >= lax.broadcasted_iota(jnp.int32, (C, C), 1))
    tri = jnp.where(tri, 1.0, 0.0).astype(BF16)
    lane = lax.broadcasted_iota(jnp.int32, (C, LANES), 1)
    first = lane < RWKV_HEAD
    rt = lax.broadcasted_iota(jnp.int32, (P, P), 0) & (C - 1)
    ct = lax.broadcasted_iota(jnp.int32, (P, P), 1) & (C - 1)
    strict = rt > ct
    incl = rt >= ct

    def stack(t):
        return jnp.concatenate([jnp.where(first, t, 0.0), jnp.where(first, 0.0, t)], axis=0)

    R = range(pairs)
    sls = [slice(p * LANES, (p + 1) * LANES) for p in R]
    lw = [lw_ref[:, sl] for sl in sls]
    lw_hi = [x.astype(BF16) for x in lw]
    lw_r1 = [lw[p] - lw_hi[p].astype(F32) for p in R]
    lw_mid = [x.astype(BF16) for x in lw_r1]
    lw_lo = [(lw_r1[p] - lw_mid[p].astype(F32)).astype(BF16) for p in R]
    cum = [_dot(tri, lw_hi[p]) + (_dot(tri, lw_mid[p]) + _dot(tri, lw_lo[p])) for p in R]
    p_inc = [jnp.exp(c) for c in cum]
    p_inv = [jnp.exp(-c) for c in cum]
    kk = [kk_ref[:, sl].astype(F32) for sl in sls]
    a_s = [stack(-kk[p] * jnp.exp(cum[p] - lw[p])).astype(BF16) for p in R]
    bk_f = [jnp.concatenate([stack(kk[p] * a_ref[:, sls[p]].astype(F32) * p_inv[p]),
                             stack(k_ref[:, sls[p]].astype(F32) * p_inv[p])], axis=0) for p in R]
    bk = [x.astype(BF16) for x in bk_f]
    bk_t = [x.T.astype(BF16) for x in bk_f]
    r_s = [stack(r_ref[:, sls[p]].astype(F32) * p_inc[p]).astype(BF16) for p in R]
    v_s = [stack(v_ref[:, sl]).astype(BF16) for sl in sls]
    st = [st_scr[p] for p in R]
    st_s = [x.astype(BF16) for x in st]
    gram = [_dot_nt(jnp.concatenate([a_s[p], r_s[p]], axis=0), bk[p]) for p in R]
    lp = [jnp.where(strict, g[:P, :P], 0.0).astype(BF16) for g in gram]
    l_ak = [jnp.where(strict, g[:P, P:], 0.0).astype(BF16) for g in gram]
    a_rb = [jnp.where(incl, g[P:, :P], 0.0).astype(BF16) for g in gram]
    a_rk = [jnp.where(incl, g[P:, P:], 0.0).astype(BF16) for g in gram]
    u = [_dot(jnp.concatenate([a_s[p], l_ak[p]], axis=1),
              jnp.concatenate([st_s[p], v_s[p]], axis=0)) for p in R]
    steps = int(math.log2(C))
    for step in range(steps):
        u_s = [x.astype(BF16) for x in u]
        if step < steps - 1:
            prod = [_dot(lp[p], jnp.concatenate([u_s[p], lp[p]], axis=1)) for p in R]
            u = [u[p] + prod[p][:, :LANES] for p in R]
            lp = [x[:, LANES:].astype(BF16) for x in prod]
        else:
            u = [u[p] + _dot(lp[p], u_s[p]) for p in R]
    u_s = [x.astype(BF16) for x in u]
    o_s = [_dot(jnp.concatenate([r_s[p], a_rb[p], a_rk[p]], axis=1),
                jnp.concatenate([st_s[p], u_s[p], v_s[p]], axis=0)) for p in R]
    upd = [_dot(bk_t[p], jnp.concatenate([u_s[p], v_s[p]], axis=0)) for p in R]
    for p in R:
        y_ref[:, sls[p]] = o_s[p][:C] + o_s[p][C:]
        decay_col = jnp.broadcast_to(p_inc[p][C - 1:C, :], (LANES, LANES)).T
        st_scr[p] = (st[p] + upd[p]) * decay_col


def _rwkv_post_kernel(x_ref, y_ref, bonus_ref, g_ref, m_ref, gnw_ref, gnb_ref, gs_ref, gst_ref,
                      w_ref, o_ref):
    y = y_ref[...]
    inv_n = 1.0 / RWKV_HEAD
    mu = _head_sum(y, gs_ref, gst_ref) * inv_n
    d = y - mu
    var = _head_sum(d * d, gs_ref, gst_ref) * inv_n
    yn = (d * lax.rsqrt(var + RWKV_GN_EPS)) * gnw_ref[...] + gnb_ref[...]
    z = ((yn + bonus_ref[...].astype(F32)) * g_ref[...].astype(F32)).astype(BF16)
    o_ref[...] = x_ref[...] + m_ref[5:6, :] * _dot(z, w_ref[...])


def _rwkv7_mixer(x, mods, g, mu, w_rkv, w0, wd1, wd2, a0, wa1, wa2, wg1, wg2, k_k, k_a, r_k,
                 gn_w, gn_b, w_out, tm=256, pairs=8):
    B, S, D = x.shape
    H, N, C = RWKV_HEADS, RWKV_HEAD, RWKV_CHUNK
    tm = min(tm, S)
    h = _modnorm_call(x, mods, g)
    gs = (jnp.arange(D)[:, None] // N == jnp.arange(LANES)[None, :]).astype(BF16)
    H = LANES

    def pad_rank(w1, w2):
        r = w1.shape[1]
        rp = -(-r // LANES) * LANES
        return (jnp.pad(w1, ((0, 0), (0, rp - r))).astype(BF16),
                jnp.pad(w2, ((0, rp - r), (0, 0))).astype(BF16))

    wd1, wd2 = pad_rank(wd1, wd2)
    wa1, wa2 = pad_rank(wa1, wa2)
    wg1, wg2 = pad_rank(wg1, wg2)
    row = lambda t: t.reshape(1, D)
    full = lambda shape: _resident(shape, lambda b, i: (0,) * len(shape))
    tile = pl.BlockSpec((None, tm, D), lambda b, i: (b, i, 0))
    outs = pl.pallas_call(
        _rwkv_prep_kernel,
        grid=(B, S // tm),
        in_specs=[
            tile,
            pl.BlockSpec((None, 8, D), lambda b, i: (b, jnp.maximum(i * (tm // 8) - 1, 0), 0)),
            full((6, D)), full((3, D, D)), full((1, D)), full(wd1.shape), full(wd2.shape),
            full((1, D)), full(wa1.shape), full(wa2.shape), full(wg1.shape), full(wg2.shape),
            full((1, D)), full((1, D)), full((1, D)), full((D, H)), full((H, D)),
        ],
        out_specs=[tile] * 8,
        out_shape=[jax.ShapeDtypeStruct((B, S, D), F32 if n == 1 else BF16) for n in range(8)],
        compiler_params=_cparams(("parallel", "parallel")),
        name="rwkv_prep",
    )(h, h, mu, w_rkv.astype(BF16), row(w0), wd1, wd2, row(a0), wa1, wa2, wg1, wg2,
      row(k_k), row(k_a), row(r_k), gs, gs.T)
    r, lw, k, v, kk, a, gg, bonus = outs
    width = pairs * LANES
    blk = pl.BlockSpec((None, C, width), lambda b, hg, c: (b, c, hg))
    y = pl.pallas_call(
        functools.partial(_rwkv_scan_kernel, pairs=pairs),
        grid=(B, D // width, S // C),
        in_specs=[blk] * 6,
        out_specs=blk,
        out_shape=jax.ShapeDtypeStruct((B, S, D), F32),
        scratch_shapes=[pltpu.VMEM((pairs, LANES, LANES), F32)],
        compiler_params=_cparams(("parallel", "parallel", "arbitrary")),
        name="rwkv_scan",
    )(r, lw, k, v, kk, a)
    tp = min(512, S)
    tile_p = pl.BlockSpec((None, tp, D), lambda b, i: (b, i, 0))
    return pl.pallas_call(
        _rwkv_post_kernel,
        grid=(B, S // tp),
        in_specs=[tile_p, tile_p, tile_p, tile_p,
                  pl.BlockSpec((None, 9, D), lambda b, i: (b, 0, 0)),
                  full((1, D)), full((1, D)), full((D, H)), full((H, D)), full((D, D))],
        out_specs=tile_p,
        out_shape=jax.ShapeDtypeStruct((B, S, D), F32),
        compiler_params=_cparams(("parallel", "parallel")),
        name="rwkv_post",
    )(x, y, bonus, gg, mods, row(gn_w), row(gn_b), gs, gs.T, w_out.astype(BF16))


def kernel(x, c, ada_w, ada_b, norm_g, ffn_w_in, ffn_w_out, final_g, rel_table, mla_w_in, mla_q_norm, mla_w_uq, mla_kv_norm, mla_w_ukv, mla_w_out, rwkv_mu, rwkv_w_rkv, rwkv_w0, rwkv_wd1, rwkv_wd2, rwkv_a0, rwkv_wa1, rwkv_wa2, rwkv_wg1, rwkv_wg2, rwkv_k_k, rwkv_k_a, rwkv_r_k, rwkv_gn_w, rwkv_gn_b, rwkv_w_out, moba_w_in, moba_w_out, ret_w_in, ret_gn_w, ret_gn_b, ret_w_out):
    depth = ada_w.shape[0]
    mods_all = _ada_mods(c, ada_w, ada_b)
    ffn_in = ffn_w_in.astype(BF16)
    ffn_out = ffn_w_out.astype(BF16)
    for i in range(depth):
        mods = mods_all[i]
        x = _ffn(x, mods, norm_g[i, 0], ffn_in, ffn_out, layer=i, which=0, sub=0)
        kind, j = i % 4, i // 4
        g = norm_g[i, 1]
        if kind == 0:
            x = _mla_mixer(x, mods, g, mla_w_in[j], mla_q_norm[j], mla_w_uq[j], mla_kv_norm[j],
                           mla_w_ukv[j], mla_w_out[j])
        elif kind == 1:
            x = _rwkv7_mixer(x, mods, g, rwkv_mu[j], rwkv_w_rkv[j], rwkv_w0[j], rwkv_wd1[j],
                             rwkv_wd2[j], rwkv_a0[j], rwkv_wa1[j], rwkv_wa2[j], rwkv_wg1[j],
                             rwkv_wg2[j], rwkv_k_k[j], rwkv_k_a[j], rwkv_r_k[j].reshape(-1),
                             rwkv_gn_w[j], rwkv_gn_b[j], rwkv_w_out[j])
        elif kind == 2:
            x = _moba_mixer(x, mods, g, moba_w_in[j], rel_table, moba_w_out[j])
        else:
            x = _retnet_mixer(x, mods, g, ret_w_in[j], ret_gn_w[j], ret_gn_b[j], ret_w_out[j])
        x = _ffn(x, mods, norm_g[i, 2], ffn_in, ffn_out, layer=i, which=1, sub=2)
    return _final_norm(x, final_g)
```

```python
import functools
import math

import jax
import jax.numpy as jnp
from jax import lax
from jax.experimental import pallas as pl
from jax.experimental.pallas import tpu as pltpu

F32 = jnp.float32
BF16 = jnp.bfloat16
HI = lax.Precision.HIGHEST

LANES = 128
VMEM_LIMIT_BYTES = 56 * 1024 * 1024

D_MODEL = 1024
DEPTH = 4
EPS = 1e-6
MACARON_W = 0.5
D_FF = 2816
ROPE_BASE = 10000.0
REL_BUCKETS = 32
REL_MAX_DIST = 128
MLA_HEADS = 16
MLA_Q_RANK = 384
MLA_KV_RANK = 256
MLA_NOPE = 64
MLA_ROPE = 32
MLA_V = 64
RWKV_HEAD = 64
RWKV_HEADS = D_MODEL // RWKV_HEAD
RWKV_GN_EPS = 64e-5
RWKV_CHUNK = 64
MOBA_HEADS = 16
MOBA_HEAD_DIM = 64
MOBA_BLOCK = 256
MOBA_TOPK = 3
RET_HEADS = 4
RET_DK = 256
RET_DV = 512
RET_CHUNK = 128
RET_GN_EPS = 1e-5
NEG_BIG = -1e30
LOG2E = math.log2(math.e)


def _cparams(sem, vmem=VMEM_LIMIT_BYTES):
    return pltpu.CompilerParams(dimension_semantics=sem, vmem_limit_bytes=vmem)


def _resident(block_shape, index_map):
    return pl.BlockSpec(block_shape, index_map, pipeline_mode=pl.Buffered(1))


def _dot(a, b):
    return jnp.dot(a, b, preferred_element_type=F32)


def _dot_hi(a, b):
    return jnp.dot(a, b, preferred_element_type=F32, precision=HI)


def _split(x):
    hi = x.astype(BF16)
    return hi, (x - hi.astype(F32)).astype(BF16)


def _dot3(a, b):
    return _dot(a[0], b[0]) + (_dot(a[0], b[1]) + _dot(a[1], b[0]))


def _dot3_nt(a, b):
    return _dot_nt(a[0], b[0]) + (_dot_nt(a[0], b[1]) + _dot_nt(a[1], b[0]))


def _dot_nt(a, b, precision=None):
    return lax.dot_general(a, b, (((1,), (1,)), ((), ())),
                           preferred_element_type=F32, precision=precision)


def _silu(x):
    return x * jax.nn.sigmoid(x)


def _modnorm(x, g, shift, scale):
    y = x * lax.rsqrt(jnp.mean(x * x, axis=-1, keepdims=True) + EPS)
    return (y * g) * (1 + scale) + shift


def _rms(x, g):
    return (x * lax.rsqrt(jnp.mean(x * x, axis=-1, keepdims=True) + EPS)) * g


def _ada_kernel(c_ref, w_ref, b_ref, o_ref):
    o_ref[...] = _dot_hi(_silu(c_ref[...]), w_ref[...]) + b_ref[...]


def _ada_mods(c, ada_w, ada_b):
    B, D = c.shape
    depth, _, N = ada_w.shape
    rows = 8
    cp = jnp.zeros((rows, D), F32).at[:B].set(c)
    tn = N // 8
    out = pl.pallas_call(
        _ada_kernel,
        grid=(depth, N // tn),
        in_specs=[
            pl.BlockSpec((rows, D), lambda l, n: (0, 0)),
            pl.BlockSpec((None, D, tn), lambda l, n: (l, 0, n)),
            pl.BlockSpec((None, 1, tn), lambda l, n: (l, 0, n)),
        ],
        out_specs=pl.BlockSpec((None, rows, tn), lambda l, n: (l, 0, n)),
        out_shape=jax.ShapeDtypeStruct((depth, rows, N), F32),
        compiler_params=_cparams(("parallel", "parallel")),
        name="ada_mods",
    )(cp, ada_w, ada_b.reshape(depth, 1, N))
    return out[:, :B].reshape(depth, B, 9, D)


def _ffn_kernel(x_ref, m_ref, g_ref, wg_ref, wu_ref, wo_ref, o_ref, *, sub, tf):
    x = x_ref[...]
    shift = m_ref[3 * sub:3 * sub + 1, :]
    scale = m_ref[3 * sub + 1:3 * sub + 2, :]
    gate = m_ref[3 * sub + 2:3 * sub + 3, :]
    h = _modnorm(x, g_ref[...], shift, scale).astype(BF16)
    acc = jnp.zeros(x.shape, F32)
    for j in range(wg_ref.shape[1] // tf):
        gt = _dot(h, wg_ref[:, j * tf:(j + 1) * tf])
        up = _dot(h, wu_ref[:, j * tf:(j + 1) * tf])
        a = (_silu(gt) * up).astype(BF16)
        acc = acc + _dot(a, wo_ref[j * tf:(j + 1) * tf, :])
    o_ref[...] = x + (MACARON_W * gate) * acc


def _ffn(x, mods, g, w_in, w_out, layer, which, sub, tm=1024, tf=256):
    B, S, D = x.shape
    dff = w_out.shape[2]
    tm = min(tm, S)
    assert dff % tf == 0
    return pl.pallas_call(
        functools.partial(_ffn_kernel, sub=sub, tf=tf),
        grid=(B, S // tm),
        in_specs=[
            pl.BlockSpec((None, tm, D), lambda b, i: (b, i, 0)),
            pl.BlockSpec((None, 9, D), lambda b, i: (b, 0, 0)),
            pl.BlockSpec((1, D), lambda b, i: (0, 0)),
            _resident((None, None, D, dff), lambda b, i: (layer, which, 0, 0)),
            _resident((None, None, D, dff), lambda b, i: (layer, which, 0, 1)),
            _resident((None, None, dff, D), lambda b, i: (layer, which, 0, 0)),
        ],
        out_specs=pl.BlockSpec((None, tm, D), lambda b, i: (b, i, 0)),
        out_shape=jax.ShapeDtypeStruct((B, S, D), F32),
        compiler_params=_cparams(("parallel", "parallel")),
        name="ffn",
    )(x, mods, g.reshape(1, D), w_in, w_in, w_out)


def _proj_kernel(x_ref, m_ref, g_ref, w_ref, *o_and_scr, sub):
    o_refs, h_scr = o_and_scr[:-1], o_and_scr[-1]

    @pl.when(pl.program_id(2) == 0)
    def _():
        h_scr[...] = _modnorm(x_ref[...], g_ref[...], m_ref[3 * sub:3 * sub + 1, :],
                              m_ref[3 * sub + 1:3 * sub + 2, :]).astype(BF16)

    y = _dot(h_scr[...], w_ref[...])
    for o_ref in o_refs:
        o_ref[...] = y.astype(o_ref.dtype)


def _proj(x, mods, g, w, out_dtypes, sub=1, tm=1024, tn=1024):
    B, S, D = x.shape
    N = w.shape[1]
    tm = min(tm, S)
    tn = tn if N % tn == 0 else N
    outs = pl.pallas_call(
        functools.partial(_proj_kernel, sub=sub),
        grid=(B, S // tm, N // tn),
        in_specs=[
            pl.BlockSpec((None, tm, D), lambda b, i, n: (b, i, 0)),
            pl.BlockSpec((None, 9, D), lambda b, i, n: (b, 0, 0)),
            pl.BlockSpec((1, D), lambda b, i, n: (0, 0)),
            pl.BlockSpec((D, tn), lambda b, i, n: (0, n)),
        ],
        out_specs=[pl.BlockSpec((None, tm, tn), lambda b, i, n: (b, i, n)) for _ in out_dtypes],
        out_shape=[jax.ShapeDtypeStruct((B, S, N), dt) for dt in out_dtypes],
        scratch_shapes=[pltpu.VMEM((tm, D), BF16)],
        compiler_params=_cparams(("parallel", "parallel", "arbitrary")),
        name="modnorm_proj",
    )(x, mods, g.reshape(1, D), w)
    return outs


def _proj_t_kernel(x_ref, m_ref, g_ref, wt_ref, *o_refs, sub):
    h = _modnorm(x_ref[...], g_ref[...], m_ref[3 * sub:3 * sub + 1, :],
                 m_ref[3 * sub + 1:3 * sub + 2, :]).astype(BF16)
    y_t = _dot_nt(wt_ref[...], h)
    for o_ref in o_refs:
        o_ref[...] = y_t.astype(o_ref.dtype)


def _proj_t(x, mods, g, wt, out_dtypes, sub=1, tm=1024):
    B, S, D = x.shape
    N = wt.shape[0]
    tm = min(tm, S)
    return pl.pallas_call(
        functools.partial(_proj_t_kernel, sub=sub),
        grid=(B, S // tm),
        in_specs=[
            pl.BlockSpec((None, tm, D), lambda b, i: (b, i, 0)),
            pl.BlockSpec((None, 9, D), lambda b, i: (b, 0, 0)),
            pl.BlockSpec((1, D), lambda b, i: (0, 0)),
            _resident((N, D), lambda b, i: (0, 0)),
        ],
        out_specs=[pl.BlockSpec((None, N, tm), lambda b, i: (b, 0, i)) for _ in out_dtypes],
        out_shape=[jax.ShapeDtypeStruct((B, N, S), dt) for dt in out_dtypes],
        compiler_params=_cparams(("parallel", "parallel")),
        name="modnorm_proj_t",
    )(x, mods, g.reshape(1, D), wt)


def _norm_kernel(x_ref, m_ref, g_ref, o_ref, *, sub):
    o_ref[...] = _modnorm(x_ref[...], g_ref[...], m_ref[3 * sub:3 * sub + 1, :],
                          m_ref[3 * sub + 1:3 * sub + 2, :])


def _modnorm_call(x, mods, g, sub=1, tm=1024):
    B, S, D = x.shape
    tm = min(tm, S)
    return pl.pallas_call(
        functools.partial(_norm_kernel, sub=sub),
        grid=(B, S // tm),
        in_specs=[
            pl.BlockSpec((None, tm, D), lambda b, i: (b, i, 0)),
            pl.BlockSpec((None, 9, D), lambda b, i: (b, 0, 0)),
            pl.BlockSpec((1, D), lambda b, i: (0, 0)),
        ],
        out_specs=pl.BlockSpec((None, tm, D), lambda b, i: (b, i, 0)),
        out_shape=jax.ShapeDtypeStruct((B, S, D), F32),
        compiler_params=_cparams(("parallel", "parallel")),
        name="modnorm",
    )(x, mods, g.reshape(1, D))


def _out_kernel(x_ref, y_ref, m_ref, w_ref, o_ref):
    o_ref[...] = x_ref[...] + m_ref[5:6, :] * _dot(y_ref[...], w_ref[...])


def _out_proj(x, y, mods, w, tm=512):
    B, S, D = x.shape
    K = w.shape[0]
    tm = min(tm, S)
    return pl.pallas_call(
        _out_kernel,
        grid=(B, S // tm),
        in_specs=[
            pl.BlockSpec((None, tm, D), lambda b, i: (b, i, 0)),
            pl.BlockSpec((None, tm, K), lambda b, i: (b, i, 0)),
            pl.BlockSpec((None, 9, D), lambda b, i: (b, 0, 0)),
            _resident((K, D), lambda b, i: (0, 0)),
        ],
        out_specs=pl.BlockSpec((None, tm, D), lambda b, i: (b, i, 0)),
        out_shape=jax.ShapeDtypeStruct((B, S, D), F32),
        compiler_params=_cparams(("parallel", "parallel")),
        name="out_proj",
    )(x, y, mods, w)


def _final_kernel(x_ref, g_ref, o_ref):
    o_ref[...] = _rms(x_ref[...], g_ref[...])


def _final_norm(x, g, tm=1024):
    B, S, D = x.shape
    tm = min(tm, S)
    return pl.pallas_call(
        _final_kernel,
        grid=(B, S // tm),
        in_specs=[pl.BlockSpec((None, tm, D), lambda b, i: (b, i, 0)),
                  pl.BlockSpec((1, D), lambda b, i: (0, 0))],
        out_specs=pl.BlockSpec((None, tm, D), lambda b, i: (b, i, 0)),
        out_shape=jax.ShapeDtypeStruct((B, S, D), F32),
        compiler_params=_cparams(("parallel", "parallel")),
        name="final_norm",
    )(x, g.reshape(1, D))


def _rope_tables(seq, dim):
    inv = ROPE_BASE ** (-jnp.arange(0, dim, 2, dtype=F32) / dim)
    ang = jnp.arange(seq, dtype=F32)[:, None] * inv[None, :]
    return jnp.cos(ang), jnp.sin(ang)


def _mla_prep_kernel(p_ref, qn_ref, kvn_ref, wqt_ref, wqst_ref, wk_ref, wvt_ref, c_ref, s_ref,
                     ct_ref, st_ref, qt_ref, k_ref, vt_ref):
    p = p_ref[...]
    cq = _rms(p[:, :MLA_Q_RANK], qn_ref[...]).astype(BF16)
    ckv = _rms(p[:, MLA_Q_RANK:MLA_Q_RANK + MLA_KV_RANK], kvn_ref[...]).astype(BF16)
    o = MLA_Q_RANK + MLA_KV_RANK
    cos = c_ref[...]
    sin = s_ref[...]
    k_rope = p[:, o:o + LANES] * cos + p[:, o + LANES:o + 2 * LANES] * sin
    q1t = _dot_nt(wqt_ref[...], cq)
    q2t = _dot_nt(wqst_ref[...], cq)
    kn = _dot(ckv, wk_ref[...])
    cos_t = ct_ref[...]
    sin_t = st_ref[...]
    scale = (MLA_NOPE + MLA_ROPE) ** -0.5 * LOG2E
    for h in range(MLA_HEADS):
        sl = slice(h * LANES, (h + 1) * LANES)
        qt_ref[sl, :] = ((q1t[sl, :] * cos_t + q2t[sl, :] * sin_t) * scale).astype(BF16)
        k_ref[:, sl] = (kn[:, sl] + k_rope).astype(BF16)
    vt_ref[...] = _dot_nt(wvt_ref[...], ckv).astype(BF16)


def _flash_update(s_t, vt, m_scr, l_scr, acc_scr, h):
    m_old = m_scr[h]
    m_new = jnp.maximum(m_old, jnp.max(s_t, axis=0, keepdims=True))
    alpha = jnp.exp2(m_old - m_new)
    p_t = jnp.exp2(s_t - m_new)
    l_scr[h] = alpha * l_scr[h] + jnp.sum(p_t, axis=0, keepdims=True)
    acc_scr[h] = alpha * acc_scr[h] + _dot(vt, p_t.astype(BF16))
    m_scr[h] = m_new


def _flash_init(m_scr, l_scr, acc_scr):
    m_scr[...] = jnp.full(m_scr.shape, -jnp.inf, F32)
    l_scr[...] = jnp.zeros(l_scr.shape, F32)
    acc_scr[...] = jnp.zeros(acc_scr.shape, F32)


def _flash_finish(o_ref, l_scr, acc_scr):
    o_t = jnp.concatenate([acc_scr[h] / l_scr[h] for h in range(acc_scr.shape[0])], axis=0)
    o_ref[...] = o_t.T.astype(o_ref.dtype)


def _mla_attn_kernel(qt_ref, k_ref, vt_ref, o_ref, s_scr, m_scr, l_scr, acc_scr, *, t):
    i = pl.program_id(2)
    nh = acc_scr.shape[0]
    _flash_init(m_scr, l_scr, acc_scr)
    q_t = qt_ref[...]

    def scores(j, slot):
        kb = k_ref[pl.ds(pl.multiple_of(j * t, t), t), :]
        for h in range(nh):
            s_scr[slot, h] = _dot(kb[:, h * LANES:(h + 1) * LANES], q_t[h * LANES:(h + 1) * LANES, :])

    def consume(j, slot, diagonal):
        vtb = vt_ref[:, pl.ds(pl.multiple_of(j * t, t), t)]
        for h in range(nh):
            s_t = s_scr[slot, h]
            if diagonal:
                key = lax.broadcasted_iota(jnp.int32, s_t.shape, 0)
                qry = lax.broadcasted_iota(jnp.int32, s_t.shape, 1)
                s_t = jnp.where(key <= qry, s_t, -jnp.inf)
            _flash_update(s_t, vtb[h * MLA_V:(h + 1) * MLA_V, :], m_scr, l_scr, acc_scr, h)

    scores(0, 0)

    def body(p, carry):
        j = 2 * p
        scores(j + 1, 1)
        consume(j, 0, False)
        scores(j + 2, 0)
        consume(j + 1, 1, False)
        return carry

    lax.fori_loop(0, i // 2, body, 0)

    @pl.when(i % 2 == 0)
    def _():
        consume(i, 0, True)

    @pl.when(i % 2 == 1)
    def _():
        scores(i, 1)
        consume(i - 1, 0, False)
        consume(i, 1, True)

    _flash_finish(o_ref, l_scr, acc_scr)


def _mla_weights(w_in, w_uq, w_ukv):
    H, dn, dr, dv = MLA_HEADS, MLA_NOPE, MLA_ROPE, MLA_V
    D = w_in.shape[0]
    o = MLA_Q_RANK + MLA_KV_RANK
    kr = w_in[:, o:]
    z64 = jnp.zeros((D, dn), F32)
    z32 = jnp.zeros((D, LANES - dn - dr), F32)
    half = dr // 2
    w_in_p = jnp.concatenate(
        [w_in[:, :o], z64, kr, z32, z64, -kr[:, half:], kr[:, :half], z32], axis=1)
    wq = w_uq.reshape(MLA_Q_RANK, H, dn + dr)
    zq = jnp.zeros((MLA_Q_RANK, H, LANES - dn - dr), F32)
    wq_p = jnp.concatenate([wq, zq], axis=2).reshape(MLA_Q_RANK, H * LANES)
    wqs_p = jnp.concatenate(
        [jnp.zeros((MLA_Q_RANK, H, dn), F32), -wq[:, :, dn + half:], wq[:, :, dn:dn + half], zq],
        axis=2).reshape(MLA_Q_RANK, H * LANES)
    wkv = w_ukv.reshape(MLA_KV_RANK, H, dn + dv)
    wk_p = jnp.concatenate(
        [wkv[:, :, :dn], jnp.zeros((MLA_KV_RANK, H, LANES - dn), F32)], axis=2).reshape(MLA_KV_RANK, H * LANES)
    wvt_p = wkv[:, :, dn:].reshape(MLA_KV_RANK, H * dv).T
    return (w_in_p.astype(BF16), wq_p.astype(BF16), wqs_p.astype(BF16),
            wk_p.astype(BF16), wvt_p.astype(BF16))


def _mla_mixer(x, mods, g, w_in, q_norm, w_uq, kv_norm, w_ukv, w_out, tm=512, t=256, nh=8):
    B, S, D = x.shape
    H = MLA_HEADS
    tm = min(tm, S)
    t = min(t, S)
    w_in_p, wq_p, wqs_p, wk_p, wvt_p = _mla_weights(w_in, w_uq, w_ukv)
    (p,) = _proj(x, mods, g, w_in_p, [F32])
    cos, sin = _rope_tables(S, MLA_ROPE)
    one = jnp.ones((S, MLA_NOPE), F32)
    pad = LANES - MLA_NOPE - MLA_ROPE
    tab_c = jnp.concatenate([one, cos, cos, jnp.ones((S, pad), F32)], axis=1)
    tab_s = jnp.concatenate([0 * one, sin, sin, jnp.zeros((S, pad), F32)], axis=1)
    NP = p.shape[-1]
    qt, k, vt = pl.pallas_call(
        _mla_prep_kernel,
        grid=(B, S // tm),
        in_specs=[
            pl.BlockSpec((None, tm, NP), lambda b, i: (b, i, 0)),
            pl.BlockSpec((1, MLA_Q_RANK), lambda b, i: (0, 0)),
            pl.BlockSpec((1, MLA_KV_RANK), lambda b, i: (0, 0)),
            _resident(wq_p.shape[::-1], lambda b, i: (0, 0)),
            _resident(wqs_p.shape[::-1], lambda b, i: (0, 0)),
            _resident(wk_p.shape, lambda b, i: (0, 0)),
            _resident(wvt_p.shape, lambda b, i: (0, 0)),
            pl.BlockSpec((tm, LANES), lambda b, i: (i, 0)),
            pl.BlockSpec((tm, LANES), lambda b, i: (i, 0)),
            pl.BlockSpec((LANES, tm), lambda b, i: (0, i)),
            pl.BlockSpec((LANES, tm), lambda b, i: (0, i)),
        ],
        out_specs=[
            pl.BlockSpec((None, H * LANES, tm), lambda b, i: (b, 0, i)),
            pl.BlockSpec((None, tm, H * LANES), lambda b, i: (b, i, 0)),
            pl.BlockSpec((None, H * MLA_V, tm), lambda b, i: (b, 0, i)),
        ],
        out_shape=[
            jax.ShapeDtypeStruct((B, H * LANES, S), BF16),
            jax.ShapeDtypeStruct((B, S, H * LANES), BF16),
            jax.ShapeDtypeStruct((B, H * MLA_V, S), BF16),
        ],
        compiler_params=_cparams(("parallel", "parallel")),
        name="mla_prep",
    )(p, q_norm.reshape(1, -1), kv_norm.reshape(1, -1), wq_p.T, wqs_p.T, wk_p, wvt_p, tab_c, tab_s,
      tab_c.T, tab_s.T)
    o = pl.pallas_call(
        functools.partial(_mla_attn_kernel, t=t),
        grid=(B, H // nh, S // t),
        in_specs=[
            pl.BlockSpec((None, nh * LANES, t), lambda b, hg, i: (b, hg, i)),
            _resident((None, S, nh * LANES), lambda b, hg, i: (b, 0, hg)),
            _resident((None, nh * MLA_V, S), lambda b, hg, i: (b, hg, 0)),
        ],
        out_specs=pl.BlockSpec((None, t, nh * MLA_V), lambda b, hg, i: (b, i, hg)),
        out_shape=jax.ShapeDtypeStruct((B, S, H * MLA_V), BF16),
        scratch_shapes=[pltpu.VMEM((2, nh, t, t), F32),
                        pltpu.VMEM((nh, 1, t), F32), pltpu.VMEM((nh, 1, t), F32),
                        pltpu.VMEM((nh, MLA_V, t), F32)],
        compiler_params=_cparams(("parallel", "parallel", "arbitrary")),
        name="mla_attn",
    )(qt, k, vt)
    return _out_proj(x, o, mods, w_out.astype(BF16))


def _moba_kmean_kernel(k_ref, o_ref):
    o_ref[...] = jnp.mean(k_ref[...], axis=0, keepdims=True)


def _moba_select_kernel(qt_ref, km_ref, o_ref, *, nb):
    H, dh, tq = MOBA_HEADS, MOBA_HEAD_DIM, MOBA_BLOCK
    own = pl.program_id(1)
    assert nb & (nb - 1) == 0 and dh & (dh - 1) == 0
    hs = 2 * LANES // dh
    shape = (hs * nb, 2 * LANES)
    row_head = lax.shift_right_logical(lax.broadcasted_iota(jnp.int32, shape, 0), nb.bit_length() - 1)
    lane_head = lax.shift_right_logical(lax.broadcasted_iota(jnp.int32, shape, 1), dh.bit_length() - 1)
    gates = []
    for c in range(H // hs):
        sl = slice(c * 2 * LANES, (c + 1) * 2 * LANES)
        kbd = jnp.concatenate([km_ref[:, sl]] * hs, axis=0)
        kbd = jnp.where(row_head == lane_head, kbd, 0.0)
        gates.append(_dot3(_split(kbd), _split(qt_ref[sl, :])))
    gate = jnp.concatenate(gates, axis=0)
    g = gate.reshape(H, nb, tq)
    n_iota = lax.broadcasted_iota(jnp.int32, g.shape, 1)
    g = jnp.where(n_iota < own, g, -jnp.inf)
    sel = jnp.zeros(g.shape, jnp.bool_)
    for _ in range(min(MOBA_TOPK, nb)):
        m = jnp.max(g, axis=1, keepdims=True)
        first = jnp.min(jnp.where(g == m, n_iota, nb), axis=1, keepdims=True)
        pick = n_iota == first
        sel = sel | (pick & (m > -jnp.inf))
        g = jnp.where(pick, -jnp.inf, g)
    bias = jnp.where(sel | (n_iota == own), 0.0, NEG_BIG)
    if nb < LANES:
        bias = jnp.concatenate([bias, jnp.zeros((H, LANES - nb, tq), F32)], axis=1)
    o_ref[...] = bias.reshape(H * LANES, tq).astype(o_ref.dtype)


def _moba_bias_kernel(tbl_ref, o_ref):
    h = pl.program_id(0)
    blk = MOBA_BLOCK
    key = lax.broadcasted_iota(jnp.int32, (blk, blk), 0)
    qry = lax.broadcasted_iota(jnp.int32, (blk, blk), 1)
    max_exact = REL_BUCKETS // 2
    for t in range(2):
        n = jnp.maximum(qry - key + blk * t, 0)
        nf = jnp.maximum(n, max_exact).astype(F32)
        large = max_exact + (jnp.log(nf / max_exact) / math.log(REL_MAX_DIST / max_exact)
                             * (REL_BUCKETS - max_exact)).astype(jnp.int32)
        large = jnp.minimum(large, REL_BUCKETS - 1)
        bucket = jnp.where(n < max_exact, n, large)
        acc = jnp.zeros((blk, blk), F32)
        for b in range(REL_BUCKETS):
            acc = jnp.where(bucket == b, tbl_ref[b, h], acc)
        o_ref[t] = acc * LOG2E


def _moba_attn_kernel(tbl_ref, qt_ref, k_ref, vt_ref, selt_ref, bias_ref, o_ref,
                      s_scr, m_scr, l_scr, acc_scr):
    hg = pl.program_id(1)
    i = pl.program_id(2)
    t = MOBA_BLOCK
    dh = MOBA_HEAD_DIM
    nh = acc_scr.shape[0]
    _flash_init(m_scr, l_scr, acc_scr)
    lane = lax.broadcasted_iota(jnp.int32, (t, LANES), 1)
    chan = lax.broadcasted_iota(jnp.int32, (LANES, t), 0)
    qa_t = []
    for h in range(nh):
        q_t = qt_ref[(h // 2) * LANES:(h // 2 + 1) * LANES, :]
        q_t = jnp.where((chan < dh) == (h % 2 == 0), q_t, 0)
        qa_t.append(jnp.concatenate([q_t, selt_ref[h * LANES:(h + 1) * LANES, :]], axis=0))

    def scores(j, slot):
        kb = k_ref[pl.ds(pl.multiple_of(j * t, t), t), :]
        onehot = jnp.where(lane == j, 1.0, 0.0).astype(BF16)
        for h in range(nh):
            ka = jnp.concatenate([kb[:, (h // 2) * LANES:(h // 2 + 1) * LANES], onehot], axis=1)
            s_scr[slot, h] = _dot(ka, qa_t[h])

    FAR, PREV, OWN = 0, 1, 2

    def consume(j, slot, kind):
        vtb = vt_ref[:, pl.ds(pl.multiple_of(j * t, t), t)]
        for h in range(nh):
            if kind == FAR:
                s_t = s_scr[slot, h] + tbl_ref[REL_BUCKETS - 1, nh * hg + h] * LOG2E
            elif kind == PREV:
                s_t = s_scr[slot, h] + bias_ref[h, 1]
            else:
                key = lax.broadcasted_iota(jnp.int32, (t, t), 0)
                qry = lax.broadcasted_iota(jnp.int32, (t, t), 1)
                s_t = jnp.where(key <= qry, s_scr[slot, h] + bias_ref[h, 0], -jnp.inf)
            _flash_update(s_t, vtb[h * dh:(h + 1) * dh, :], m_scr, l_scr, acc_scr, h)

    scores(0, 0)
    n_far = jnp.maximum(i - 1, 0)

    def body(p, carry):
        j = 2 * p
        scores(j + 1, 1)
        consume(j, 0, FAR)
        scores(j + 2, 0)
        consume(j + 1, 1, FAR)
        return carry

    lax.fori_loop(0, n_far // 2, body, 0)

    @pl.when(i == 0)
    def _():
        consume(i, 0, OWN)

    @pl.when(i % 2 == 1)
    def _():
        scores(i, 1)
        consume(i - 1, 0, PREV)
        consume(i, 1, OWN)

    @pl.when((i % 2 == 0) & (i > 0))
    def _():
        scores(i - 1, 1)
        consume(i - 2, 0, FAR)
        scores(i, 0)
        consume(i - 1, 1, PREV)
        consume(i, 0, OWN)

    _flash_finish(o_ref, l_scr, acc_scr)


def _moba_mixer(x, mods, g, w_in, rel_table, w_out, nh=8):
    B, S, D = x.shape
    H, t = MOBA_HEADS, MOBA_BLOCK
    nb = S // t
    q_scale = MOBA_HEAD_DIM ** -0.5 * LOG2E
    qt, qt16 = _proj_t(x, mods, g, (w_in[:, :D] * q_scale).T.astype(BF16), [F32, BF16])
    k, k16 = _proj(x, mods, g, w_in[:, D:2 * D].astype(BF16), [F32, BF16])
    (vt,) = _proj_t(x, mods, g, w_in[:, 2 * D:].T.astype(BF16), [BF16])
    kmean = pl.pallas_call(
        _moba_kmean_kernel,
        grid=(B, nb),
        in_specs=[pl.BlockSpec((None, t, D), lambda b, n: (b, n, 0))],
        out_specs=pl.BlockSpec((None, None, 1, D), lambda b, n: (b, n, 0, 0)),
        out_shape=jax.ShapeDtypeStruct((B, nb, 1, D), F32),
        compiler_params=_cparams(("parallel", "parallel")),
        name="moba_kmean",
    )(k).reshape(B, nb, D)
    sel = pl.pallas_call(
        functools.partial(_moba_select_kernel, nb=nb),
        grid=(B, nb),
        in_specs=[pl.BlockSpec((None, D, t), lambda b, i: (b, 0, i)),
                  pl.BlockSpec((None, nb, D), lambda b, i: (b, 0, 0))],
        out_specs=pl.BlockSpec((None, H * LANES, t), lambda b, i: (b, 0, i)),
        out_shape=jax.ShapeDtypeStruct((B, H * LANES, S), BF16),
        compiler_params=_cparams(("parallel", "parallel")),
        name="moba_select",
    )(qt, kmean)
    bias = pl.pallas_call(
        _moba_bias_kernel,
        grid=(H,),
        in_specs=[pl.BlockSpec(memory_space=pltpu.SMEM)],
        out_specs=pl.BlockSpec((None, 2, t, t), lambda h: (h, 0, 0, 0)),
        out_shape=jax.ShapeDtypeStruct((H, 2, t, t), F32),
        compiler_params=_cparams(("parallel",)),
        name="moba_bias",
    )(rel_table)
    ng = H // nh
    w = nh * MOBA_HEAD_DIM
    o = pl.pallas_call(
        _moba_attn_kernel,
        grid=(B, ng, nb),
        in_specs=[
            pl.BlockSpec(memory_space=pltpu.SMEM),
            pl.BlockSpec((None, w, t), lambda b, hg, i: (b, hg, i)),
            _resident((None, S, w), lambda b, hg, i: (b, 0, hg)),
            _resident((None, w, S), lambda b, hg, i: (b, hg, 0)),
            pl.BlockSpec((None, nh * LANES, t), lambda b, hg, i: (b, hg, i)),
            pl.BlockSpec((nh, 2, t, t), lambda b, hg, i: (hg, 0, 0, 0)),
        ],
        out_specs=pl.BlockSpec((None, t, w), lambda b, hg, i: (b, i, hg)),
        out_shape=jax.ShapeDtypeStruct((B, S, D), BF16),
        scratch_shapes=[pltpu.VMEM((2, nh, t, t), F32),
                        pltpu.VMEM((nh, 1, t), F32), pltpu.VMEM((nh, 1, t), F32),
                        pltpu.VMEM((nh, MOBA_HEAD_DIM, t), F32)],
        compiler_params=_cparams(("parallel", "parallel", "arbitrary")),
        name="moba_attn",
    )(rel_table, qt16, k16, vt, sel, bias)
    return _out_proj(x, o, mods, w_out.astype(BF16))


def _ret_kernel(gam_ref, q_ref, k_ref, v_ref, g_ref, cos_ref, sin_ref, dm_ref, zeta_ref, xi_ref,
                gnw_ref, gnb_ref, o_ref, r_scr):
    @pl.when(pl.program_id(1) == 0)
    def _():
        r_scr[...] = jnp.zeros(r_scr.shape, F32)

    cos = cos_ref[...]
    sin = sin_ref[...]
    DK, DV = RET_DK, RET_DV
    half = DK // 2
    heads = range(RET_HEADS)

    def rope(t):
        t1, t2 = t[:, :half], t[:, half:]
        return jnp.concatenate([t1 * cos - t2 * sin, t1 * sin + t2 * cos], axis=1)

    q16 = [rope(q_ref[:, h * DK:(h + 1) * DK]).astype(BF16) for h in heads]
    k = [rope(k_ref[:, h * DK:(h + 1) * DK] * DK ** -0.5) for h in heads]
    v16 = [v_ref[:, h * DV:(h + 1) * DV].astype(BF16) for h in heads]
    R = [r_scr[h] for h in heads]
    inner = [(_dot_nt(q16[h], k[h].astype(BF16)) * dm_ref[h]).astype(BF16) for h in heads]
    cross = [_dot(q16[h], R[h].astype(BF16)) for h in heads]
    o = [_dot(inner[h], v16[h]) + cross[h] * xi_ref[h] for h in heads]
    kv = [_dot((k[h] * zeta_ref[h]).T.astype(BF16), v16[h]) for h in heads]
    for h in heads:
        r_scr[h] = gam_ref[h] * R[h] + kv[h]
        mu = jnp.mean(o[h], axis=-1, keepdims=True)
        d = o[h] - mu
        var = jnp.mean(d * d, axis=-1, keepdims=True)
        sl = slice(h * DV, (h + 1) * DV)
        yn = (d * lax.rsqrt(var + RET_GN_EPS)) * gnw_ref[:, sl] + gnb_ref[:, sl]
        o_ref[:, sl] = (_silu(g_ref[:, sl]) * yn).astype(o_ref.dtype)


def _retnet_mixer(x, mods, g, w_in, gn_w, gn_b, w_out):
    B, S, D = x.shape
    H, DK, DV, C = RET_HEADS, RET_DK, RET_DV, RET_CHUNK
    C = min(C, S)
    (p,) = _proj(x, mods, g, w_in.astype(BF16), [F32])
    cos, sin = _rope_tables(S, DK)
    log_gamma = jnp.log1p(-jnp.exp2(-5.0 - jnp.arange(H, dtype=F32)))
    idx = jnp.arange(C, dtype=F32)
    diff = idx[:, None] - idx[None, :]
    dmask = jnp.where(diff >= 0, jnp.exp(jnp.maximum(diff, 0.0)[None] * log_gamma[:, None, None]), 0.0)
    zeta = jnp.exp((C - 1 - idx)[None, :] * log_gamma[:, None])[:, :, None]
    xi = jnp.exp((idx + 1)[None, :] * log_gamma[:, None])[:, :, None]
    gamma_c = jnp.exp(C * log_gamma)
    y = pl.pallas_call(
        _ret_kernel,
        grid=(B, S // C),
        in_specs=[
            pl.BlockSpec(memory_space=pltpu.SMEM),
            pl.BlockSpec((None, C, D), lambda b, c: (b, c, 0)),
            pl.BlockSpec((None, C, D), lambda b, c: (b, c, 1)),
            pl.BlockSpec((None, C, 2 * D), lambda b, c: (b, c, 1)),
            pl.BlockSpec((None, C, 2 * D), lambda b, c: (b, c, 2)),
            pl.BlockSpec((C, DK // 2), lambda b, c: (c, 0)),
            pl.BlockSpec((C, DK // 2), lambda b, c: (c, 0)),
            pl.BlockSpec((H, C, C), lambda b, c: (0, 0, 0)),
            pl.BlockSpec((H, C, 1), lambda b, c: (0, 0, 0)),
            pl.BlockSpec((H, C, 1), lambda b, c: (0, 0, 0)),
            pl.BlockSpec((1, H * DV), lambda b, c: (0, 0)),
            pl.BlockSpec((1, H * DV), lambda b, c: (0, 0)),
        ],
        out_specs=pl.BlockSpec((None, C, H * DV), lambda b, c: (b, c, 0)),
        out_shape=jax.ShapeDtypeStruct((B, S, H * DV), BF16),
        scratch_shapes=[pltpu.VMEM((H, DK, DV), F32)],
        compiler_params=_cparams(("parallel", "arbitrary")),
        name="retention",
    )(gamma_c, p, p, p, p, cos, sin, dmask, zeta, xi, gn_w.reshape(1, -1), gn_b.reshape(1, -1))
    return _out_proj(x, y, mods, w_out.astype(BF16))


def _head_sum(t, gs_ref, gst_ref):
    def times_indicator(x, m):
        hi, lo = _split(x)
        return _dot(hi, m) + _dot(lo, m)

    return times_indicator(times_indicator(t, gs_ref[...]), gst_ref[...])


def _rwkv_prep_kernel(h_ref, hp_ref, mu_ref, wrkv_ref, w0_ref, wd1_ref, wd2_ref, a0_ref, wa1_ref,
                      wa2_ref, wg1_ref, wg2_ref, kk_ref, ka_ref, rk_ref, gs_ref, gst_ref,
                      r_o, lw_o, k_o, v_o, kk_o, a_o, g_o, bonus_o):
    i = pl.program_id(1)
    h = h_ref[...]
    tm = h.shape[0]
    prev_tail = jnp.where(i > 0, hp_ref[7:8, :], 0.0)
    rolled = pltpu.roll(h, 1, axis=0)
    row = lax.broadcasted_iota(jnp.int32, h.shape, 0)
    xx = jnp.where(row == 0, prev_tail, rolled) - h
    xs = [h + xx * mu_ref[n:n + 1, :] for n in range(6)]
    r = _dot(xs[0].astype(BF16), wrkv_ref[0])
    k = _dot(xs[1].astype(BF16), wrkv_ref[1])
    v = _dot(xs[2].astype(BF16), wrkv_ref[2])
    def low_rank(t, w1_ref, w2_ref, act):
        return _dot(act(_dot(t.astype(BF16), w1_ref[...])).astype(BF16), w2_ref[...])

    z = -(w0_ref[...] + low_rank(xs[3], wd1_ref, wd2_ref, jnp.tanh))
    softplus = jnp.maximum(z, 0.0) + jnp.log(1.0 + jnp.exp(-jnp.abs(z)))
    w = -softplus - 0.5
    lw_o[...] = -jnp.exp(w)
    a = jax.nn.sigmoid(a0_ref[...] + low_rank(xs[4], wa1_ref, wa2_ref, lambda t: t))
    g_o[...] = low_rank(xs[5], wg1_ref, wg2_ref, jax.nn.sigmoid).astype(g_o.dtype)
    kk = k * kk_ref[...]
    norm = jnp.sqrt(_head_sum(kk * kk, gs_ref, gst_ref))
    kk_o[...] = (kk / jnp.maximum(norm, 1e-12)).astype(kk_o.dtype)
    k_mod = k * (1 + (a - 1) * ka_ref[...])
    r_o[...] = r.astype(r_o.dtype)
    k_o[...] = k_mod.astype(k_o.dtype)
    v_o[...] = v.astype(v_o.dtype)
    a_o[...] = a.astype(a_o.dtype)
    bonus_o[...] = (_head_sum(r * k_mod * rk_ref[...], gs_ref, gst_ref) * v).astype(bonus_o.dtype)


def _rwkv_scan_kernel(r_ref, lw_ref, k_ref, v_ref, kk_ref, a_ref, y_ref, st_scr, *, pairs):
    C = RWKV_CHUNK
    P = 2 * C

    @pl.when(pl.program_id(2) == 0)
    def _():
        st_scr[...] = jnp.zeros(st_scr.shape, F32)

    tri = (lax.broadcasted_iota(jnp.int32, (C, C), 0) >= lax.broadcasted_iota(jnp.int32, (C, C), 1))
    tri = jnp.where(tri, 1.0, 0.0).astype(BF16)
    lane = lax.broadcasted_iota(jnp.int32, (C, LANES), 1)
    first = lane < RWKV_HEAD
    rt = lax.broadcasted_iota(jnp.int32, (P, P), 0) & (C - 1)
    ct = lax.broadcasted_iota(jnp.int32, (P, P), 1) & (C - 1)
    strict = rt > ct
    incl = rt >= ct

    def stack(t):
        return jnp.concatenate([jnp.where(first, t, 0.0), jnp.where(first, 0.0, t)], axis=0)

    R = range(pairs)
    sls = [slice(p * LANES, (p + 1) * LANES) for p in R]
    lw = [lw_ref[:, sl] for sl in sls]
    lw_hi = [x.astype(BF16) for x in lw]
    lw_r1 = [lw[p] - lw_hi[p].astype(F32) for p in R]
    lw_mid = [x.astype(BF16) for x in lw_r1]
    lw_lo = [(lw_r1[p] - lw_mid[p].astype(F32)).astype(BF16) for p in R]
    cum = [_dot(tri, lw_hi[p]) + (_dot(tri, lw_mid[p]) + _dot(tri, lw_lo[p])) for p in R]
    p_inc = [jnp.exp(c) for c in cum]
    p_inv = [jnp.exp(-c) for c in cum]
    kk = [kk_ref[:, sl].astype(F32) for sl in sls]
    a_s = [stack(-kk[p] * jnp.exp(cum[p] - lw[p])).astype(BF16) for p in R]
    bk_f = [jnp.concatenate([stack(kk[p] * a_ref[:, sls[p]].astype(F32) * p_inv[p]),
                             stack(k_ref[:, sls[p]].astype(F32) * p_inv[p])], axis=0) for p in R]
    bk = [x.astype(BF16) for x in bk_f]
    bk_t = [x.T.astype(BF16) for x in bk_f]
    r_s = [stack(r_ref[:, sls[p]].astype(F32) * p_inc[p]).astype(BF16) for p in R]
    v_s = [stack(v_ref[:, sl]).astype(BF16) for sl in sls]
    st = [st_scr[p] for p in R]
    st_s = [x.astype(BF16) for x in st]
    gram = [_dot_nt(jnp.concatenate([a_s[p], r_s[p]], axis=0), bk[p]) for p in R]
    lp = [jnp.where(strict, g[:P, :P], 0.0).astype(BF16) for g in gram]
    l_ak = [jnp.where(strict, g[:P, P:], 0.0).astype(BF16) for g in gram]
    a_rb = [jnp.where(incl, g[P:, :P], 0.0).astype(BF16) for g in gram]
    a_rk = [jnp.where(incl, g[P:, P:], 0.0).astype(BF16) for g in gram]
    u = [_dot(jnp.concatenate([a_s[p], l_ak[p]], axis=1),
              jnp.concatenate([st_s[p], v_s[p]], axis=0)) for p in R]
    steps = int(math.log2(C))
    for step in range(steps):
        u_s = [x.astype(BF16) for x in u]
        if step < steps - 1:
            prod = [_dot(lp[p], jnp.concatenate([u_s[p], lp[p]], axis=1)) for p in R]
            u = [u[p] + prod[p][:, :LANES] for p in R]
            lp = [x[:, LANES:].astype(BF16) for x in prod]
        else:
            u = [u[p] + _dot(lp[p], u_s[p]) for p in R]
    u_s = [x.astype(BF16) for x in u]
    o_s = [_dot(jnp.concatenate([r_s[p], a_rb[p], a_rk[p]], axis=1),
                jnp.concatenate([st_s[p], u_s[p], v_s[p]], axis=0)) for p in R]
    upd = [_dot(bk_t[p], jnp.concatenate([u_s[p], v_s[p]], axis=0)) for p in R]
    for p in R:
        y_ref[:, sls[p]] = o_s[p][:C] + o_s[p][C:]
        decay_col = jnp.broadcast_to(p_inc[p][C - 1:C, :], (LANES, LANES)).T
        st_scr[p] = (st[p] + upd[p]) * decay_col


def _rwkv_post_kernel(x_ref, y_ref, bonus_ref, g_ref, m_ref, gnw_ref, gnb_ref, gs_ref, gst_ref,
                      w_ref, o_ref):
    y = y_ref[...]
    inv_n = 1.0 / RWKV_HEAD
    mu = _head_sum(y, gs_ref, gst_ref) * inv_n
    d = y - mu
    var = _head_sum(d * d, gs_ref, gst_ref) * inv_n
    yn = (d * lax.rsqrt(var + RWKV_GN_EPS)) * gnw_ref[...] + gnb_ref[...]
    z = ((yn + bonus_ref[...].astype(F32)) * g_ref[...].astype(F32)).astype(BF16)
    o_ref[...] = x_ref[...] + m_ref[5:6, :] * _dot(z, w_ref[...])


def _rwkv7_mixer(x, mods, g, mu, w_rkv, w0, wd1, wd2, a0, wa1, wa2, wg1, wg2, k_k, k_a, r_k,
                 gn_w, gn_b, w_out, tm=256, pairs=8):
    B, S, D = x.shape
    H, N, C = RWKV_HEADS, RWKV_HEAD, RWKV_CHUNK
    tm = min(tm, S)
    h = _modnorm_call(x, mods, g)
    gs = (jnp.arange(D)[:, None] // N == jnp.arange(LANES)[None, :]).astype(BF16)
    H = LANES

    def pad_rank(w1, w2):
        r = w1.shape[1]
        rp = -(-r // LANES) * LANES
        return (jnp.pad(w1, ((0, 0), (0, rp - r))).astype(BF16),
                jnp.pad(w2, ((0, rp - r), (0, 0))).astype(BF16))

    wd1, wd2 = pad_rank(wd1, wd2)
    wa1, wa2 = pad_rank(wa1, wa2)
    wg1, wg2 = pad_rank(wg1, wg2)
    row = lambda t: t.reshape(1, D)
    full = lambda shape: _resident(shape, lambda b, i: (0,) * len(shape))
    tile = pl.BlockSpec((None, tm, D), lambda b, i: (b, i, 0))
    outs = pl.pallas_call(
        _rwkv_prep_kernel,
        grid=(B, S // tm),
        in_specs=[
            tile,
            pl.BlockSpec((None, 8, D), lambda b, i: (b, jnp.maximum(i * (tm // 8) - 1, 0), 0)),
            full((6, D)), full((3, D, D)), full((1, D)), full(wd1.shape), full(wd2.shape),
            full((1, D)), full(wa1.shape), full(wa2.shape), full(wg1.shape), full(wg2.shape),
            full((1, D)), full((1, D)), full((1, D)), full((D, H)), full((H, D)),
        ],
        out_specs=[tile] * 8,
        out_shape=[jax.ShapeDtypeStruct((B, S, D), F32 if n == 1 else BF16) for n in range(8)],
        compiler_params=_cparams(("parallel", "parallel")),
        name="rwkv_prep",
    )(h, h, mu, w_rkv.astype(BF16), row(w0), wd1, wd2, row(a0), wa1, wa2, wg1, wg2,
      row(k_k), row(k_a), row(r_k), gs, gs.T)
    r, lw, k, v, kk, a, gg, bonus = outs
    width = pairs * LANES
    blk = pl.BlockSpec((None, C, width), lambda b, hg, c: (b, c, hg))
    y = pl.pallas_call(
        functools.partial(_rwkv_scan_kernel, pairs=pairs),
        grid=(B, D // width, S // C),
        in_specs=[blk] * 6,
        out_specs=blk,
        out_shape=jax.ShapeDtypeStruct((B, S, D), F32),
        scratch_shapes=[pltpu.VMEM((pairs, LANES, LANES), F32)],
        compiler_params=_cparams(("parallel", "parallel", "arbitrary")),
        name="rwkv_scan",
    )(r, lw, k, v, kk, a)
    tp = min(512, S)
    tile_p = pl.BlockSpec((None, tp, D), lambda b, i: (b, i, 0))
    return pl.pallas_call(
        _rwkv_post_kernel,
        grid=(B, S // tp),
        in_specs=[tile_p, tile_p, tile_p, tile_p,
                  pl.BlockSpec((None, 9, D), lambda b, i: (b, 0, 0)),
                  full((1, D)), full((1, D)), full((D, H)), full((H, D)), full((D, D))],
        out_specs=tile_p,
        out_shape=jax.ShapeDtypeStruct((B, S, D), F32),
        compiler_params=_cparams(("parallel", "parallel")),
        name="rwkv_post",
    )(x, y, bonus, gg, mods, row(gn_w), row(gn_b), gs, gs.T, w_out.astype(BF16))


def kernel(x, c, ada_w, ada_b, norm_g, ffn_w_in, ffn_w_out, final_g, rel_table, mla_w_in, mla_q_norm, mla_w_uq, mla_kv_norm, mla_w_ukv, mla_w_out, rwkv_mu, rwkv_w_rkv, rwkv_w0, rwkv_wd1, rwkv_wd2, rwkv_a0, rwkv_wa1, rwkv_wa2, rwkv_wg1, rwkv_wg2, rwkv_k_k, rwkv_k_a, rwkv_r_k, rwkv_gn_w, rwkv_gn_b, rwkv_w_out, moba_w_in, moba_w_out, ret_w_in, ret_gn_w, ret_gn_b, ret_w_out):
    depth = ada_w.shape[0]
    mods_all = _ada_mods(c, ada_w, ada_b)
    ffn_in = ffn_w_in.astype(BF16)
    ffn_out = ffn_w_out.astype(BF16)
    for i in range(depth):
        mods = mods_all[i]
        x = _ffn(x, mods, norm_g[i, 0], ffn_in, ffn_out, layer=i, which=0, sub=0)
        kind, j = i % 4, i // 4
        g = norm_g[i, 1]
        if kind == 0:
            x = _mla_mixer(x, mods, g, mla_w_in[j], mla_q_norm[j], mla_w_uq[j], mla_kv_norm[j],
                           mla_w_ukv[j], mla_w_out[j])
        elif kind == 1:
            x = _rwkv7_mixer(x, mods, g, rwkv_mu[j], rwkv_w_rkv[j], rwkv_w0[j], rwkv_wd1[j],
                             rwkv_wd2[j], rwkv_a0[j], rwkv_wa1[j], rwkv_wa2[j], rwkv_wg1[j],
                             rwkv_wg2[j], rwkv_k_k[j], rwkv_k_a[j], rwkv_r_k[j].reshape(-1),
                             rwkv_gn_w[j], rwkv_gn_b[j], rwkv_w_out[j])
        elif kind == 2:
            x = _moba_mixer(x, mods, g, moba_w_in[j], rel_table, moba_w_out[j])
        else:
            x = _retnet_mixer(x, mods, g, ret_w_in[j], ret_gn_w[j], ret_gn_b[j], ret_w_out[j])
        x = _ffn(x, mods, norm_g[i, 2], ffn_in, ffn_out, layer=i, which=1, sub=2)
    return _final_norm(x, final_g)
```
